```python
import math
import jax
import jax.numpy as jnp
from jax import lax
import numpy as np

D_MODEL = 1024
BATCH = 8
SEQ = 2048
DEPTH = 4
DEC_BATCH = 128
DEC_SEQ = 1
PAST_LEN = 8192
PAGE_SIZE = 128

N_MIXERS = 2
EPS = 1e-6
Q_BLOCK = 128
ROPE_THETA = 10000.0
MLA_HEADS = 16
MLA_NOPE = 64
MLA_ROPE = 32
MLA_QK = MLA_NOPE + MLA_ROPE
MLA_V = 64
Q_LORA = 384
KV_LORA = 256
MLA_IN = Q_LORA + KV_LORA + MLA_ROPE
MLA_SCALE = MLA_QK ** -0.5
FOX_HEADS = 16
FOX_KV_HEADS = 4
FOX_GROUP = FOX_HEADS // FOX_KV_HEADS
FOX_HD = 64
FOX_IN = 2 * FOX_HEADS * FOX_HD + 2 * FOX_KV_HEADS * FOX_HD + FOX_HEADS
FOX_SCALE = FOX_HD ** -0.5
D_FF = 2816
N_EXPERTS = 8
TOP_K = 2
MOE_FF = 3584

N_MLA = (DEPTH + 1) // 2
N_FOX = DEPTH // 2
N_DENSE = (DEPTH + 1) // 2
N_MOE = DEPTH // 2

kernel_name = 'mla_fox_hybrid_decoder_step'


def rmsnorm(x, g):
    xf = x.astype(jnp.float32)
    xf = xf * lax.rsqrt(jnp.mean(xf * xf, axis=-1, keepdims=True) + EPS)
    return (xf * g.astype(jnp.float32)).astype(x.dtype)


def rope(x, pos):
    half = x.shape[-1] // 2
    inv = 1.0 / (ROPE_THETA ** (jnp.arange(half, dtype=jnp.float32) / half))
    ang = pos.astype(jnp.float32)[..., None] * inv
    cos = jnp.cos(ang)[..., None, :]
    sin = jnp.sin(ang)[..., None, :]
    xf = x.astype(jnp.float32)
    x1, x2 = xf[..., :half], xf[..., half:]
    return jnp.concatenate([x1 * cos - x2 * sin, x1 * sin + x2 * cos], axis=-1).astype(x.dtype)


def mla_project(h, pos, w_in, g_qlat, g_kvlat, w_qup, g_qn):
    lead = h.shape[:-1]
    z = h @ w_in
    cq = rmsnorm(z[..., :Q_LORA], g_qlat)
    ckv = rmsnorm(z[..., Q_LORA:Q_LORA + KV_LORA], g_kvlat)
    kpe = z[..., Q_LORA + KV_LORA:]
    q = rmsnorm((cq @ w_qup).reshape(*lead, MLA_HEADS, MLA_QK), g_qn)
    q = jnp.concatenate([q[..., :MLA_NOPE], rope(q[..., MLA_NOPE:], pos)], axis=-1)
    return q, ckv, kpe


def mla_keys(ckv, kpe, pos, w_uk, w_uv, g_kn):
    lead = ckv.shape[:-1]
    k_nope = (ckv @ w_uk).reshape(*lead, MLA_HEADS, MLA_NOPE)
    k_pe = jnp.broadcast_to(kpe[..., None, :], (*lead, MLA_HEADS, MLA_ROPE))
    k = rmsnorm(jnp.concatenate([k_nope, k_pe], axis=-1), g_kn)
    k = jnp.concatenate([k[..., :MLA_NOPE], rope(k[..., MLA_NOPE:], pos)], axis=-1)
    v = (ckv @ w_uv).reshape(*lead, MLA_HEADS, MLA_V)
    return k, v


def fox_project(h, w_in, b_f, g_qn, g_kn):
    lead = h.shape[:-1]
    nq = FOX_HEADS * FOX_HD
    nk = FOX_KV_HEADS * FOX_HD
    z = h @ w_in
    q = rmsnorm(z[..., :nq].reshape(*lead, FOX_HEADS, FOX_HD), g_qn)
    k = rmsnorm(z[..., nq:nq + nk].reshape(*lead, FOX_KV_HEADS, FOX_HD), g_kn)
    v = z[..., nq + nk:nq + 2 * nk].reshape(*lead, FOX_KV_HEADS, FOX_HD)
    gate = z[..., nq + 2 * nk:2 * nq + 2 * nk]
    logf = jax.nn.log_sigmoid(z[..., 2 * nq + 2 * nk:].astype(jnp.float32) + b_f.astype(jnp.float32))
    return q, k, v, gate, logf


def _heads_to_groups(b, n_kv):
    B, T, H = b.shape
    return b.reshape(B, T, n_kv, H // n_kv).transpose(0, 2, 3, 1)


def causal_attention_blocks(q, k, v, scale, fbias=None):
    B, S, H, Dk = q.shape
    Hk = k.shape[2]
    G = H // Hk
    Dv = v.shape[-1]
    kpos = jnp.arange(S)
    fk = None if fbias is None else _heads_to_groups(fbias, Hk)[..., None, :]

    def one_block(i):
        start = i * Q_BLOCK
        qb = lax.dynamic_slice_in_dim(q, start, Q_BLOCK, axis=1).reshape(B, Q_BLOCK, Hk, G, Dk)
        s = jnp.einsum('btkgd,bskd->bkgts', qb, k).astype(jnp.float32) * scale
        if fbias is not None:
            fq = _heads_to_groups(lax.dynamic_slice_in_dim(fbias, start, Q_BLOCK, axis=1), Hk)[..., None]
            s = s + (fq - fk)
        qpos = start + jnp.arange(Q_BLOCK)
        s = jnp.where(kpos[None, :] <= qpos[:, None], s, -jnp.inf)
        p = jax.nn.softmax(s, axis=-1).astype(v.dtype)
        return jnp.einsum('bkgts,bskd->btkgd', p, v).reshape(B, Q_BLOCK, H, Dv)

    o = lax.map(one_block, jnp.arange(S // Q_BLOCK))
    return o.transpose(1, 0, 2, 3, 4).reshape(B, S, H, Dv)


def _online_init(DB, Hk, G, T, Dv):
    return (jnp.full((DB, Hk, G, T), -jnp.inf, jnp.float32),
            jnp.zeros((DB, Hk, G, T), jnp.float32),
            jnp.zeros((DB, Hk, G, T, Dv), jnp.float32))


def _online_update(carry, s, v):
    m, l, acc = carry
    m_new = jnp.maximum(m, jnp.max(s, axis=-1))
    alpha = jnp.exp(m - m_new)
    p = jnp.exp(s - m_new[..., None])
    l = l * alpha + jnp.sum(p, axis=-1)
    acc = acc * alpha[..., None] + jnp.einsum('bkgts,bskd->bkgtd', p, v.astype(jnp.float32))
    return (m_new, l, acc)


def _online_finish(carry, dtype):
    m, l, acc = carry
    o = acc / l[..., None]
    DB, Hk, G, T, Dv = o.shape
    return o.transpose(0, 3, 1, 2, 4).reshape(DB, T, Hk * G, Dv).astype(dtype)


def mla_decode(q, ckv_new, kpe_new, pos_new, cache_ckv, cache_kpe, layer, page_table, w_uk, w_uv, g_kn):
    DB, T = q.shape[:2]
    n_pages = page_table.shape[1]
    qg = q[:, :, :, None, :]

    def page_step(carry, xs):
        p_idx, phys = xs
        pos = p_idx * PAGE_SIZE + jnp.arange(PAGE_SIZE)
        k, v = mla_keys(cache_ckv[layer, phys], cache_kpe[layer, phys], pos, w_uk, w_uv, g_kn)
        s = jnp.einsum('btkgd,bskd->bkgts', qg, k).astype(jnp.float32) * MLA_SCALE
        return _online_update(carry, s, v), None

    carry = _online_init(DB, MLA_HEADS, 1, T, MLA_V)
    carry, _ = lax.scan(page_step, carry, (jnp.arange(n_pages), page_table.T))
    k, v = mla_keys(ckv_new, kpe_new, pos_new, w_uk, w_uv, g_kn)
    s = jnp.einsum('btkgd,bskd->bkgts', qg, k).astype(jnp.float32) * MLA_SCALE
    causal = jnp.arange(T)[None, :] <= jnp.arange(T)[:, None]
    carry = _online_update(carry, jnp.where(causal, s, -jnp.inf), v)
    return _online_finish(carry, q.dtype)


def fox_decode(q, k_new, v_new, logf_new, cache_k, cache_v, cache_lf, layer, page_table):
    DB, T, H, D = q.shape
    Hk, G = FOX_KV_HEADS, FOX_GROUP
    n_pages = page_table.shape[1]
    lf_past = cache_lf[layer, page_table].astype(jnp.float32).reshape(DB, n_pages * PAGE_SIZE, H)
    suffix = lax.cumsum(lf_past, axis=1, reverse=True) - lf_past
    suffix = suffix.reshape(DB, n_pages, PAGE_SIZE, H).transpose(1, 0, 2, 3)
    f_new = jnp.cumsum(logf_new, axis=1)
    fq = _heads_to_groups(f_new, Hk)[..., None]
    qg = q.reshape(DB, T, Hk, G, D)

    def page_step(carry, xs):
        phys, suf = xs
        s = jnp.einsum('btkgd,bskd->bkgts', qg, cache_k[layer, phys]).astype(jnp.float32) * FOX_SCALE
        s = s + fq + _heads_to_groups(suf, Hk)[..., None, :]
        return _online_update(carry, s, cache_v[layer, phys]), None

    carry = _online_init(DB, Hk, G, T, D)
    carry, _ = lax.scan(page_step, carry, (page_table.T, suffix))
    s = jnp.einsum('btkgd,bskd->bkgts', qg, k_new).astype(jnp.float32) * FOX_SCALE
    s = s + fq - _heads_to_groups(f_new, Hk)[..., None, :]
    causal = jnp.arange(T)[None, :] <= jnp.arange(T)[:, None]
    carry = _online_update(carry, jnp.where(causal, s, -jnp.inf), v_new)
    return _online_finish(carry, q.dtype)


def swiglu(h, w_gu, w_down):
    g, u = jnp.split(h @ w_gu, 2, axis=-1)
    return (jax.nn.silu(g) * u) @ w_down


def moe_swiglu(h, w_router, w_gu, w_down):
    logits = (h @ w_router).astype(jnp.float32)
    top_v, top_i = lax.top_k(logits, TOP_K)
    wts = jax.nn.softmax(top_v, axis=-1)
    gates = jnp.sum(jax.nn.one_hot(top_i, N_EXPERTS, dtype=jnp.float32) * wts[..., None], axis=-2).astype(h.dtype)
    y = jnp.zeros_like(h)
    for e in range(N_EXPERTS):
        y = y + gates[..., e:e + 1] * swiglu(h, w_gu[e], w_down[e])
    return y


def setup_inputs(seed: int = 0) -> dict:
    key = jax.random.key(seed)
    ks = jax.random.split(key, 40)
    f32 = jnp.float32
    n_pages = PAST_LEN // PAGE_SIZE
    n_used = DEC_BATCH * n_pages
    n_phys = n_used + max(1, n_used // 4)

    def w(k, shape, fan_in):
        return jax.random.normal(k, shape, f32) * fan_in ** -0.5

    def gain(k, shape):
        return 1.0 + 0.05 * jax.random.normal(k, shape, f32)

    page_table = jax.random.permutation(ks[0], n_phys)[:n_used].reshape(DEC_BATCH, n_pages).astype(jnp.int32)
    lf_off = jax.random.uniform(ks[1], (N_FOX, 1, 1, FOX_HEADS), f32, 2.0, 8.0)
    cache_fox_logf = jax.nn.log_sigmoid(jax.random.normal(ks[2], (N_FOX, n_phys, PAGE_SIZE, FOX_HEADS), f32) + lf_off)
    return {
        'x_prompt': jax.random.normal(ks[3], (BATCH, SEQ, D_MODEL), f32),
        'x_sample': jax.random.normal(ks[4], (DEC_BATCH, DEC_SEQ, D_MODEL), f32),
        'cache_mla_ckv': jax.random.normal(ks[5], (N_MLA, n_phys, PAGE_SIZE, KV_LORA), f32),
        'cache_mla_kpe': jax.random.normal(ks[6], (N_MLA, n_phys, PAGE_SIZE, MLA_ROPE), f32),
        'cache_fox_k': jax.random.normal(ks[7], (N_FOX, n_phys, PAGE_SIZE, FOX_KV_HEADS, FOX_HD), f32),
        'cache_fox_v': jax.random.normal(ks[8], (N_FOX, n_phys, PAGE_SIZE, FOX_KV_HEADS, FOX_HD), f32),
        'cache_fox_logf': cache_fox_logf,
        'page_table': page_table,
        'norm_mix': gain(ks[9], (DEPTH, D_MODEL)),
        'norm_ffn': gain(ks[10], (DEPTH, D_MODEL)),
        'mla_w_in': w(ks[11], (N_MLA, D_MODEL, MLA_IN), D_MODEL),
        'mla_g_qlat': gain(ks[12], (N_MLA, Q_LORA)),
        'mla_g_kvlat': gain(ks[13], (N_MLA, KV_LORA)),
        'mla_w_qup': w(ks[14], (N_MLA, Q_LORA, MLA_HEADS * MLA_QK), Q_LORA),
        'mla_w_uk': w(ks[15], (N_MLA, KV_LORA, MLA_HEADS * MLA_NOPE), KV_LORA),
        'mla_w_uv': w(ks[16], (N_MLA, KV_LORA, MLA_HEADS * MLA_V), KV_LORA),
        'mla_g_qn': gain(ks[17], (N_MLA, MLA_QK)),
        'mla_g_kn': gain(ks[18], (N_MLA, MLA_QK)),
        'mla_w_o': w(ks[19], (N_MLA, MLA_HEADS * MLA_V, D_MODEL), MLA_HEADS * MLA_V),
        'fox_w_in': w(ks[20], (N_FOX, D_MODEL, FOX_IN), D_MODEL),
        'fox_b_f': jax.random.uniform(ks[21], (N_FOX, FOX_HEADS), f32, 2.0, 8.0),
        'fox_g_qn': gain(ks[22], (N_FOX, FOX_HD)),
        'fox_g_kn': gain(ks[23], (N_FOX, FOX_HD)),
        'fox_w_o': w(ks[24], (N_FOX, FOX_HEADS * FOX_HD, D_MODEL), FOX_HEADS * FOX_HD),
        'ffn_w_gu': w(ks[25], (N_DENSE, D_MODEL, 2 * D_FF), D_MODEL),
        'ffn_w_down': w(ks[26], (N_DENSE, D_FF, D_MODEL), D_FF),
        'moe_w_router': w(ks[27], (N_MOE, D_MODEL, N_EXPERTS), D_MODEL),
        'moe_w_gu': w(ks[28], (N_MOE, N_EXPERTS, D_MODEL, 2 * MOE_FF), D_MODEL),
        'moe_w_down': w(ks[29], (N_MOE, N_EXPERTS, MOE_FF, D_MODEL), MOE_FF),
    }


def reference(x_prompt, x_sample, cache_mla_ckv, cache_mla_kpe, cache_fox_k, cache_fox_v, cache_fox_logf,
              page_table, norm_mix, norm_ffn, mla_w_in, mla_g_qlat, mla_g_kvlat, mla_w_qup, mla_w_uk, mla_w_uv,
              mla_g_qn, mla_g_kn, mla_w_o, fox_w_in, fox_b_f, fox_g_qn, fox_g_kn, fox_w_o, ffn_w_gu, ffn_w_down,
              moe_w_router, moe_w_gu, moe_w_down):
    B, S, _ = x_prompt.shape
    DB, T, _ = x_sample.shape
    past_len = page_table.shape[1] * PAGE_SIZE
    pos_p = jnp.arange(S)
    pos_s = past_len + jnp.arange(T)
    xp, xs = x_prompt, x_sample
    p_ckv, p_kpe, s_ckv, s_kpe = [], [], [], []
    p_fk, p_fv, p_flf, s_fk, s_fv, s_flf = [], [], [], [], [], []
    for i in range(DEPTH):
        j = i // N_MIXERS
        hp = rmsnorm(xp, norm_mix[i])
        hs = rmsnorm(xs, norm_mix[i])
        if i % N_MIXERS == 0:
            qp, ckvp, kpep = mla_project(hp, pos_p, mla_w_in[j], mla_g_qlat[j], mla_g_kvlat[j], mla_w_qup[j], mla_g_qn[j])
            kp, vp = mla_keys(ckvp, kpep, pos_p, mla_w_uk[j], mla_w_uv[j], mla_g_kn[j])
            op = causal_attention_blocks(qp, kp, vp, MLA_SCALE)
            qs, ckvs, kpes = mla_project(hs, pos_s, mla_w_in[j], mla_g_qlat[j], mla_g_kvlat[j], mla_w_qup[j], mla_g_qn[j])
            os_ = mla_decode(qs, ckvs, kpes, pos_s, cache_mla_ckv, cache_mla_kpe, j, page_table,
                             mla_w_uk[j], mla_w_uv[j], mla_g_kn[j])
            xp = xp + op.reshape(B, S, -1) @ mla_w_o[j]
            xs = xs + os_.reshape(DB, T, -1) @ mla_w_o[j]
            p_ckv.append(ckvp)
            p_kpe.append(kpep)
            s_ckv.append(ckvs)
            s_kpe.append(kpes)
        else:
            qp, kp, vp, gp, lfp = fox_project(hp, fox_w_in[j], fox_b_f[j], fox_g_qn[j], fox_g_kn[j])
            op = causal_attention_blocks(qp, kp, vp, FOX_SCALE, jnp.cumsum(lfp, axis=1))
            qs, ks_, vs, gs, lfs = fox_project(hs, fox_w_in[j], fox_b_f[j], fox_g_qn[j], fox_g_kn[j])
            os_ = fox_decode(qs, ks_, vs, lfs, cache_fox_k, cache_fox_v, cache_fox_logf, j, page_table)
            xp = xp + (op.reshape(B, S, -1) * jax.nn.sigmoid(gp)) @ fox_w_o[j]
            xs = xs + (os_.reshape(DB, T, -1) * jax.nn.sigmoid(gs)) @ fox_w_o[j]
            p_fk.append(kp)
            p_fv.append(vp)
            p_flf.append(lfp)
            s_fk.append(ks_)
            s_fv.append(vs)
            s_flf.append(lfs)
        hp = rmsnorm(xp, norm_ffn[i])
        hs = rmsnorm(xs, norm_ffn[i])
        if i % 2 == 0:
            xp = xp + swiglu(hp, ffn_w_gu[j], ffn_w_down[j])
            xs = xs + swiglu(hs, ffn_w_gu[j], ffn_w_down[j])
        else:
            xp = xp + moe_swiglu(hp, moe_w_router[j], moe_w_gu[j], moe_w_down[j])
            xs = xs + moe_swiglu(hs, moe_w_router[j], moe_w_gu[j], moe_w_down[j])
    return (xp, xs, jnp.stack(p_ckv), jnp.stack(p_kpe), jnp.stack(p_fk), jnp.stack(p_fv), jnp.stack(p_flf),
            jnp.stack(s_ckv), jnp.stack(s_kpe), jnp.stack(s_fk), jnp.stack(s_fv), jnp.stack(s_flf))
```

```python
import functools

import numpy as np
import jax
import jax.numpy as jnp
from jax import lax
from jax.experimental import pallas as pl
from jax.experimental.pallas import tpu as pltpu

F32 = jnp.float32
BF16 = jnp.bfloat16

D_MODEL = 1024
EPS = 1e-6
ROPE_THETA = 10000.0
PAGE = 128
MLA_H, MLA_NOPE, MLA_ROPE, MLA_V = 16, 64, 32, 64
MLA_QK = MLA_NOPE + MLA_ROPE
Q_LORA, KV_LORA = 384, 256
MLA_SCALE = MLA_QK ** -0.5
FOX_H, FOX_HK, FOX_HD = 16, 4, 64
FOX_G = FOX_H // FOX_HK
FOX_SCALE = FOX_HD ** -0.5
N_EXP = 8

LANES = 128
VMEM_LIMIT = 56 * 1024 * 1024
HB = LANES
PG = 8


def _cparams(sem):
    return pltpu.CompilerParams(dimension_semantics=sem, vmem_limit_bytes=VMEM_LIMIT)


def _rms(x, g):
    return x * lax.rsqrt(jnp.mean(x * x, axis=-1, keepdims=True) + EPS) * g


def _dot(a, b):
    return jnp.dot(a, b, preferred_element_type=F32)


def _dot_nt(a, b):
    return lax.dot_general(a, b, (((1,), (1,)), ((), ())), preferred_element_type=F32)


def _dot_tn(a, b):
    return lax.dot_general(a, b, (((0,), (0,)), ((), ())), preferred_element_type=F32)


def _mla_lane_of_dim():
    lane = np.zeros(MLA_QK, np.int32)
    for d in range(MLA_QK):
        if d < 32:
            lane[d] = d
        elif d < 64:
            lane[d] = 64 + (d - 32)
        elif d < 80:
            lane[d] = 32 + (d - 64)
        else:
            lane[d] = 96 + (d - 80)
    return lane


MLA_LANE = _mla_lane_of_dim()
MLA_ROPE_LANES = np.concatenate([32 + np.arange(16), 96 + np.arange(16)])


def _inv_index(lane_of_dim, n_dims):
    inv = np.full(HB, n_dims, np.int32)
    inv[lane_of_dim] = np.arange(len(lane_of_dim))
    return inv


def _pad_heads(w, n_heads, hd, lane_of_dim):
    k = w.shape[0]
    w3 = w.reshape(k, n_heads, hd)
    w3 = jnp.concatenate([w3, jnp.zeros((k, n_heads, 1), w.dtype)], axis=-1)
    return w3[:, :, _inv_index(lane_of_dim, hd)].reshape(k, n_heads * HB)


def _pad_vec(g, lane_of_dim):
    g1 = jnp.concatenate([g, jnp.zeros((1,), g.dtype)])
    return g1[_inv_index(lane_of_dim, g.shape[0])].reshape(1, HB)


def _rope_tables(pos):
    half = MLA_ROPE // 2
    inv = 1.0 / (ROPE_THETA ** (jnp.arange(half, dtype=F32) / half))
    ang = pos.astype(F32)[:, None] * inv
    cos, sin = jnp.cos(ang), jnp.sin(ang)
    n = pos.shape[0]
    c = jnp.ones((n, HB), F32).at[:, 32:48].set(cos).at[:, 96:112].set(cos)
    s = jnp.zeros((n, HB), F32).at[:, 32:48].set(-sin).at[:, 96:112].set(sin)
    return c, s


def _mla_in_kernel(x_ref, g_ref, w_ref, gq_ref, gkv_ref, cq_ref, ckv_ref, kpl_ref, kpe_ref):
    xn = _rms(x_ref[...], g_ref[...])
    z = _dot(xn.astype(BF16), w_ref[...])
    cq_ref[...] = _rms(z[:, :Q_LORA], gq_ref[...]).astype(BF16)
    ckv_ref[...] = _rms(z[:, Q_LORA:Q_LORA + KV_LORA], gkv_ref[...])
    kpl_ref[...] = z[:, 640:768]
    kpe_ref[...] = z[:, 768:800]


def mla_in(x, g_mix, w_ext, g_qlat, g_kvlat, tm):
    t = x.shape[0]
    row = lambda n: pl.BlockSpec((tm, n), lambda i: (i, 0))
    full = lambda a: pl.BlockSpec(a.shape, lambda i: (0,) * a.ndim)
    return pl.pallas_call(
        _mla_in_kernel, grid=(t // tm,),
        in_specs=[row(D_MODEL), full(g_mix), full(w_ext), full(g_qlat), full(g_kvlat)],
        out_specs=[row(Q_LORA), row(KV_LORA), row(HB), row(MLA_ROPE)],
        out_shape=[jax.ShapeDtypeStruct((t, Q_LORA), BF16), jax.ShapeDtypeStruct((t, KV_LORA), F32),
                   jax.ShapeDtypeStruct((t, HB), F32), jax.ShapeDtypeStruct((t, MLA_ROPE), F32)],
        compiler_params=_cparams(("parallel",)), name="mla_in")(x, g_mix, w_ext, g_qlat, g_kvlat)


def _head_norm_rope(zh, g, c, s, dim, scale):
    ss = jnp.sum(zh * zh, axis=-1, keepdims=True)
    zn = zh * lax.rsqrt(ss * (1.0 / dim) + EPS) * g
    out = zn * c + pltpu.roll(zn, HB // 2, 1) * s
    return out * scale if scale != 1.0 else out


def _mla_q_kernel(cq_ref, w_ref, g_ref, c_ref, s_ref, q_ref):
    z = _dot(cq_ref[...], w_ref[...])
    g, c, s = g_ref[...], c_ref[...], s_ref[...]
    for h in range(MLA_H):
        sl = slice(h * HB, (h + 1) * HB)
        q_ref[:, sl] = _head_norm_rope(z[:, sl], g, c, s, MLA_QK, MLA_SCALE).astype(BF16)


def mla_q(cq, wq, gq_lay, ctab, stab, tm):
    t = cq.shape[0]
    nb = ctab.shape[0] // tm
    row = lambda n: pl.BlockSpec((tm, n), lambda i: (i, 0))
    tab = pl.BlockSpec((tm, HB), lambda i: (i % nb, 0))
    full = lambda a: pl.BlockSpec(a.shape, lambda i: (0,) * a.ndim)
    return pl.pallas_call(
        _mla_q_kernel, grid=(t // tm,),
        in_specs=[row(Q_LORA), full(wq), full(gq_lay), tab, tab],
        out_specs=row(MLA_H * HB),
        out_shape=jax.ShapeDtypeStruct((t, MLA_H * HB), BF16),
        compiler_params=_cparams(("parallel",)), name="mla_q")(cq, wq, gq_lay, ctab, stab)


def _mla_kv_kernel(ckv_ref, kpl_ref, wk_ref, wv_ref, g_ref, c_ref, s_ref, k_ref, v_ref):
    cb = ckv_ref[...].astype(BF16)
    kn = _dot(cb, wk_ref[...])
    v_ref[...] = _dot(cb, wv_ref[...]).astype(BF16)
    kpl, g, c, s = kpl_ref[...], g_ref[...], c_ref[...], s_ref[...]
    for h in range(MLA_H):
        sl = slice(h * HB, (h + 1) * HB)
        k_ref[:, sl] = _head_norm_rope(kn[:, sl] + kpl, g, c, s, MLA_QK, 1.0).astype(BF16)


def mla_kv(ckv, kpl, wk, wv, gk_lay, ctab, stab, tm):
    t = ckv.shape[0]
    nb = ctab.shape[0] // tm
    row = lambda n: pl.BlockSpec((tm, n), lambda i: (i, 0))
    tab = pl.BlockSpec((tm, HB), lambda i: (i % nb, 0))
    full = lambda a: pl.BlockSpec(a.shape, lambda i: (0,) * a.ndim)
    return pl.pallas_call(
        _mla_kv_kernel, grid=(t // tm,),
        in_specs=[row(KV_LORA), row(HB), full(wk), full(wv), full(gk_lay), tab, tab],
        out_specs=[row(MLA_H * HB), row(MLA_H * HB)],
        out_shape=[jax.ShapeDtypeStruct((t, MLA_H * HB), BF16)] * 2,
        compiler_params=_cparams(("parallel",)), name="mla_kv")(ckv, kpl, wk, wv, gk_lay, ctab, stab)


def _flash_kernel(q_ref, k_ref, v_ref, o_ref, *, tq):
    qi = pl.program_id(2)
    q = q_ref[...]

    def step(j, carry, masked):
        m, l, acc = carry
        start = pl.multiple_of(j * tq, tq)
        kj = k_ref[pl.ds(start, tq), :]
        vj = v_ref[pl.ds(start, tq), :]
        s = _dot_nt(q, kj)
        if masked:
            r = lax.broadcasted_iota(jnp.int32, (tq, tq), 0)
            c = lax.broadcasted_iota(jnp.int32, (tq, tq), 1)
            s = jnp.where(c <= r, s, -jnp.inf)
        m_new = jnp.maximum(m, jnp.max(s, axis=-1, keepdims=True))
        alpha = jnp.exp(m - m_new)
        p = jnp.exp(s - m_new)
        l = l * alpha + jnp.sum(p, axis=-1, keepdims=True)
        acc = acc * alpha + _dot(p.astype(BF16), vj)
        return m_new, l, acc

    init = (jnp.full((tq, 1), -jnp.inf, F32), jnp.zeros((tq, 1), F32), jnp.zeros((tq, HB), F32))
    carry = lax.fori_loop(0, qi, lambda j, c: step(j, c, False), init)
    m, l, acc = step(qi, carry, True)
    o_ref[...] = (acc / l).astype(BF16)


def flash_attention(q, k, v, batch, seq, n_heads, n_kv, tq):
    nq = seq // tq
    grp = n_heads // n_kv
    return pl.pallas_call(
        functools.partial(_flash_kernel, tq=tq), grid=(batch, n_heads, nq),
        in_specs=[pl.BlockSpec((tq, HB), lambda b, h, i: (b * nq + i, h)),
                  pl.BlockSpec((seq, HB), lambda b, h, i: (b, h // grp)),
                  pl.BlockSpec((seq, HB), lambda b, h, i: (b, h // grp))],
        out_specs=pl.BlockSpec((tq, HB), lambda b, h, i: (b * nq + i, h)),
        out_shape=jax.ShapeDtypeStruct((batch * seq, n_heads * HB), BF16),
        compiler_params=_cparams(("parallel", "parallel", "arbitrary")), name="flash")(q, k, v)


def _out_proj_kernel(x_ref, o_ref, w_ref, y_ref):
    y_ref[...] = x_ref[...] + _dot(o_ref[...], w_ref[...])


def _out_proj_gate_kernel(x_ref, o_ref, g_ref, w_ref, y_ref):
    og = (o_ref[...].astype(F32) * g_ref[...].astype(F32)).astype(BF16)
    y_ref[...] = x_ref[...] + _dot(og, w_ref[...])


def out_proj(x, o, w, tm, gate=None):
    t = x.shape[0]
    n = o.shape[1]
    row = lambda c: pl.BlockSpec((tm, c), lambda i: (i, 0))
    wspec = pl.BlockSpec(w.shape, lambda i: (0, 0))
    if gate is None:
        kern, args, specs = _out_proj_kernel, (x, o, w), [row(D_MODEL), row(n), wspec]
    else:
        kern, args, specs = _out_proj_gate_kernel, (x, o, gate, w), [row(D_MODEL), row(n), row(n), wspec]
    return pl.pallas_call(
        kern, grid=(t // tm,), in_specs=specs, out_specs=row(D_MODEL),
        out_shape=jax.ShapeDtypeStruct((t, D_MODEL), F32),
        compiler_params=_cparams(("parallel",)), name="out_proj")(*args)


def _swiglu_chunk(hb, wg, wu, wd):
    g = _dot(hb, wg)
    u = _dot(hb, wu)
    a = (g * jax.nn.sigmoid(g) * u).astype(BF16)
    return _dot(a, wd)


def _ffn_kernel(x_ref, g_ref, wg_ref, wu_ref, wd_ref, y_ref, h_scr, acc_scr):
    c = pl.program_id(1)

    @pl.when(c == 0)
    def _():
        x = x_ref[...]
        h_scr[...] = _rms(x, g_ref[...]).astype(BF16)
        acc_scr[...] = x

    acc_scr[...] += _swiglu_chunk(h_scr[...], wg_ref[...], wu_ref[...], wd_ref[...])

    @pl.when(c == pl.num_programs(1) - 1)
    def _():
        y_ref[...] = acc_scr[...]


def ffn_dense(x, g, w_gu, w_down, tm, fc):
    t = x.shape[0]
    dff = w_down.shape[0]
    nc = dff // fc
    return pl.pallas_call(
        _ffn_kernel, grid=(t // tm, nc),
        in_specs=[pl.BlockSpec((tm, D_MODEL), lambda i, c: (i, 0)),
                  pl.BlockSpec((1, D_MODEL), lambda i, c: (0, 0)),
                  pl.BlockSpec((D_MODEL, fc), lambda i, c: (0, c)),
                  pl.BlockSpec((D_MODEL, fc), lambda i, c: (0, nc + c)),
                  pl.BlockSpec((fc, D_MODEL), lambda i, c: (c, 0))],
        out_specs=pl.BlockSpec((tm, D_MODEL), lambda i, c: (i, 0)),
        out_shape=jax.ShapeDtypeStruct((t, D_MODEL), F32),
        scratch_shapes=[pltpu.VMEM((tm, D_MODEL), BF16), pltpu.VMEM((tm, D_MODEL), F32)],
        compiler_params=_cparams(("parallel", "arbitrary")), name="ffn_dense")(x, g, w_gu, w_gu, w_down)


def _expert_kernel(te_ref, tv_ref, xs_ref, wg_ref, wu_ref, wd_ref, ys_ref, h_scr, acc_scr):
    i = pl.program_id(0)
    c = pl.program_id(1)

    @pl.when(tv_ref[i] > 0)
    def _():
        @pl.when(c == 0)
        def _():
            h_scr[...] = xs_ref[...].astype(BF16)
            acc_scr[...] = jnp.zeros_like(acc_scr)

        acc_scr[...] += _swiglu_chunk(h_scr[...], wg_ref[0], wu_ref[0], wd_ref[0])

        @pl.when(c == pl.num_programs(1) - 1)
        def _():
            ys_ref[...] = acc_scr[...]

    @pl.when((tv_ref[i] == 0) & (c == 0))
    def _():
        ys_ref[...] = jnp.zeros_like(ys_ref)


def moe_experts(tile_expert, tile_valid, xs, w_gu, w_down, tme, fc):
    npad = xs.shape[0]
    dff = w_down.shape[1]
    nc = dff // fc
    last = nc - 1
    cc = lambda i, c, tv: jnp.where(tv[i] > 0, c, last)
    grid_spec = pltpu.PrefetchScalarGridSpec(
        num_scalar_prefetch=2, grid=(npad // tme, nc),
        in_specs=[pl.BlockSpec((tme, D_MODEL), lambda i, c, te, tv: (i, 0)),
                  pl.BlockSpec((1, D_MODEL, fc), lambda i, c, te, tv: (te[i], 0, cc(i, c, tv))),
                  pl.BlockSpec((1, D_MODEL, fc), lambda i, c, te, tv: (te[i], 0, nc + cc(i, c, tv))),
                  pl.BlockSpec((1, fc, D_MODEL), lambda i, c, te, tv: (te[i], cc(i, c, tv), 0))],
        out_specs=pl.BlockSpec((tme, D_MODEL), lambda i, c, te, tv: (i, 0)),
        scratch_shapes=[pltpu.VMEM((tme, D_MODEL), BF16), pltpu.VMEM((tme, D_MODEL), F32)])
    return pl.pallas_call(
        _expert_kernel, grid_spec=grid_spec,
        out_shape=jax.ShapeDtypeStruct((npad, D_MODEL), F32),
        compiler_params=_cparams(("arbitrary", "arbitrary")), name="moe_experts")(
            tile_expert, tile_valid, xs, w_gu, w_gu, w_down)


def _router_kernel(x_ref, g_ref, wr_ref, tri_ref, h_ref, meta_ref, cnt_ref, carry):
    i = pl.program_id(0)

    @pl.when(i == 0)
    def _():
        carry[...] = jnp.zeros_like(carry)

    h = _rms(x_ref[...], g_ref[...])
    h_ref[...] = h
    logits = jnp.dot(h, wr_ref[...], preferred_element_type=F32, precision=lax.Precision.HIGHEST)
    tm = logits.shape[0]
    lane = lax.broadcasted_iota(jnp.int32, (tm, LANES), 1).astype(F32)
    logits = jnp.where(lane < N_EXP, logits, -jnp.inf)
    m1 = jnp.max(logits, axis=-1, keepdims=True)
    i1 = jnp.min(jnp.where(logits == m1, lane, float(LANES)), axis=-1, keepdims=True)
    rest = jnp.where(lane == i1, -jnp.inf, logits)
    m2 = jnp.max(rest, axis=-1, keepdims=True)
    i2 = jnp.min(jnp.where(rest == m2, lane, float(LANES)), axis=-1, keepdims=True)
    e = jnp.exp(m2 - m1)
    w1 = 1.0 / (1.0 + e)
    w2 = e * w1
    hot = ((lane == i1) | (lane == i2)).astype(F32)
    before = _dot(tri_ref[...], hot.astype(BF16)) + carry[...]
    r1 = jnp.sum(jnp.where(lane == i1, before, 0.0), axis=-1, keepdims=True)
    r2 = jnp.sum(jnp.where(lane == i2, before, 0.0), axis=-1, keepdims=True)
    carry[...] += jnp.sum(hot, axis=0, keepdims=True)
    cnt_ref[...] = jnp.broadcast_to(carry[...], cnt_ref.shape)
    meta = jnp.where(lane == 0, i1, 0.0)
    meta = jnp.where(lane == 1, i2, meta)
    meta = jnp.where(lane == 2, w1, meta)
    meta = jnp.where(lane == 3, w2, meta)
    meta = jnp.where(lane == 4, r1, meta)
    meta = jnp.where(lane == 5, r2, meta)
    meta_ref[...] = meta


def moe_router(x, g, wr_pad, tm):
    t = x.shape[0]
    tri = jnp.asarray(np.tril(np.ones((tm, tm), np.float32), -1), BF16)
    return pl.pallas_call(
        _router_kernel, grid=(t // tm,),
        in_specs=[pl.BlockSpec((tm, D_MODEL), lambda i: (i, 0)),
                  pl.BlockSpec((1, D_MODEL), lambda i: (0, 0)),
                  pl.BlockSpec((D_MODEL, LANES), lambda i: (0, 0)),
                  pl.BlockSpec((tm, tm), lambda i: (0, 0))],
        out_specs=[pl.BlockSpec((tm, D_MODEL), lambda i: (i, 0)),
                   pl.BlockSpec((tm, LANES), lambda i: (i, 0)),
                   pl.BlockSpec((8, LANES), lambda i: (0, 0))],
        out_shape=[jax.ShapeDtypeStruct((t, D_MODEL), F32), jax.ShapeDtypeStruct((t, LANES), F32),
                   jax.ShapeDtypeStruct((8, LANES), F32)],
        scratch_shapes=[pltpu.VMEM((1, LANES), F32)],
        compiler_params=_cparams(("arbitrary",)), name="moe_router")(x, g, wr_pad, tri)


def _row_copy(src_hbm, dst_ref, src_row, dst_row, sem):
    return pltpu.make_async_copy(src_hbm.at[pl.ds(src_row, 1)], dst_ref.at[pl.ds(dst_row, 1)], sem)


def _dispatch_kernel(src_ref, h_hbm, xs_hbm, sem, *, rows):
    base = pl.program_id(0) * rows

    def issue(r, _):
        _row_copy(h_hbm, xs_hbm, src_ref[0, 0, r], base + r, sem).start()
        return 0

    lax.fori_loop(0, rows, issue, 0, unroll=8)

    def drain(r, _):
        _row_copy(h_hbm, xs_hbm, 0, base + r, sem).wait()
        return 0

    lax.fori_loop(0, rows, drain, 0, unroll=8)


def moe_dispatch(src, h, rows):
    npad = src.shape[0]
    src3 = src.reshape(npad // rows, 1, rows)
    return pl.pallas_call(
        functools.partial(_dispatch_kernel, rows=rows), grid=(npad // rows,),
        in_specs=[pl.BlockSpec((1, 1, rows), lambda i: (i, 0, 0), memory_space=pltpu.SMEM),
                  pl.BlockSpec(memory_space=pl.ANY)],
        out_specs=pl.BlockSpec(memory_space=pl.ANY),
        out_shape=jax.ShapeDtypeStruct((npad, D_MODEL), F32),
        scratch_shapes=[pltpu.SemaphoreType.DMA(())],
        compiler_params=_cparams(("arbitrary",)), name="moe_dispatch")(src3, h)


def _combine_kernel(pos_ref, x_ref, meta_ref, ys_hbm, y_ref, buf, sem, *, rows):
    def issue(r, _):
        _row_copy(ys_hbm, buf.at[0], pos_ref[0, 0, r], r, sem).start()
        _row_copy(ys_hbm, buf.at[1], pos_ref[0, 1, r], r, sem).start()
        return 0

    lax.fori_loop(0, rows, issue, 0, unroll=8)

    def drain(r, _):
        _row_copy(ys_hbm, buf.at[0], 0, r, sem).wait()
        _row_copy(ys_hbm, buf.at[1], 0, r, sem).wait()
        return 0

    lax.fori_loop(0, rows, drain, 0, unroll=8)
    meta = meta_ref[...]
    y_ref[...] = x_ref[...] + meta[:, 2:3] * buf[0] + meta[:, 3:4] * buf[1]


def moe_combine(pos, x, meta, ys, rows):
    t = x.shape[0]
    pos3 = pos.T.reshape(2, t // rows, rows).transpose(1, 0, 2)
    return pl.pallas_call(
        functools.partial(_combine_kernel, rows=rows), grid=(t // rows,),
        in_specs=[pl.BlockSpec((1, 2, rows), lambda i: (i, 0, 0), memory_space=pltpu.SMEM),
                  pl.BlockSpec((rows, D_MODEL), lambda i: (i, 0)),
                  pl.BlockSpec((rows, LANES), lambda i: (i, 0)),
                  pl.BlockSpec(memory_space=pl.ANY)],
        out_specs=pl.BlockSpec((rows, D_MODEL), lambda i: (i, 0)),
        out_shape=jax.ShapeDtypeStruct((t, D_MODEL), F32),
        scratch_shapes=[pltpu.VMEM((2, rows, D_MODEL), F32), pltpu.SemaphoreType.DMA(())],
        compiler_params=_cparams(("arbitrary",)), name="moe_combine")(pos3, x, meta, ys)


def moe_layer(x, g, wr_pad, w_gu, w_down, tm, tme, fc):
    t = x.shape[0]
    h, meta, cnt = moe_router(x, g, wr_pad, tm)
    ids = meta[:, 0:2].astype(jnp.int32)
    rank = meta[:, 4:6].astype(jnp.int32)
    counts = cnt[0, :N_EXP].astype(jnp.int32)
    padded = ((counts + tme - 1) // tme) * tme
    ends = jnp.cumsum(padded)
    offs = ends - padded
    pos = offs[ids] + rank
    n_tiles = (2 * t) // tme + N_EXP
    npad = n_tiles * tme
    tile_start = jnp.arange(n_tiles, dtype=jnp.int32) * tme
    tile_valid = (tile_start < ends[-1]).astype(jnp.int32)
    last_start = jnp.maximum(ends[-1] - tme, 0)
    tile_expert = jnp.searchsorted(ends, jnp.minimum(tile_start, last_start), side="right").astype(jnp.int32)
    tile_expert = jnp.minimum(tile_expert, N_EXP - 1)
    tok = jnp.repeat(jnp.arange(t, dtype=jnp.int32), 2)
    src = jnp.zeros((npad,), jnp.int32).at[pos.reshape(-1)].set(tok)
    xs = moe_dispatch(src, h, tme)
    ys = moe_experts(tile_expert, tile_valid, xs, w_gu, w_down, tme, fc)
    return moe_combine(pos, x, meta, ys, min(256, t))


def _fox_lf_kernel(x_ref, g_ref, w_ref, b_ref, tri_ref, lf_ref, fp_ref, carry, *, cumulative):
    xn = _rms(x_ref[...], g_ref[...])
    z = _dot(xn.astype(BF16), w_ref[...]) + b_ref[...]
    lf = jnp.minimum(z, 0.0) - jnp.log1p(jnp.exp(-jnp.abs(z)))
    lf_ref[...] = lf
    if cumulative:
        @pl.when(pl.program_id(1) == 0)
        def _():
            carry[...] = jnp.zeros_like(carry)

        hi = lf.astype(BF16)
        r1 = lf - hi.astype(F32)
        mid = r1.astype(BF16)
        lo = (r1 - mid.astype(F32)).astype(BF16)
        tri = tri_ref[...]
        f = _dot(tri, hi) + _dot(tri, mid) + _dot(tri, lo) + carry[...]
        carry[...] = f[-1:, :]
    else:
        f = lf
    hi = f.astype(BF16)
    r1 = f - hi.astype(F32)
    mid = r1.astype(BF16)
    lo = (r1 - mid.astype(F32)).astype(BF16)
    lane = lax.broadcasted_iota(jnp.int32, f.shape, 1)
    fp_ref[...] = jnp.where(lane < 16, hi, jnp.where(lane < 32, mid, lo))


def fox_lf(x, g_mix, wf3, bf3, batch, seq, tm, cumulative):
    ns = seq // tm
    tri = jnp.asarray(np.tril(np.ones((tm, tm), np.float32)), BF16)
    row = lambda n: pl.BlockSpec((tm, n), lambda b, i: (b * ns + i, 0))
    full = lambda a: pl.BlockSpec(a.shape, lambda b, i: (0,) * a.ndim)
    t = batch * seq
    return pl.pallas_call(
        functools.partial(_fox_lf_kernel, cumulative=cumulative), grid=(batch, ns),
        in_specs=[row(D_MODEL), full(g_mix), full(wf3), full(bf3), full(tri)],
        out_specs=[row(LANES), row(LANES)],
        out_shape=[jax.ShapeDtypeStruct((t, LANES), F32), jax.ShapeDtypeStruct((t, LANES), BF16)],
        scratch_shapes=[pltpu.VMEM((1, LANES), F32)],
        compiler_params=_cparams(("arbitrary", "arbitrary")), name="fox_lf")(x, g_mix, wf3, bf3, tri)


def _fox_proj_kernel(x_ref, g_ref, wq_ref, wk_ref, wv_ref, wg_ref, gq_ref, gk_ref, fp_ref,
                     pq_ref, pk_ref, oq_ref, ok_ref,
                     q_ref, k_ref, kf_ref, v_ref, vf_ref, gate_ref):
    xb = _rms(x_ref[...], g_ref[...]).astype(BF16)
    fp = fp_ref[...]
    zq = _dot(xb, wq_ref[...])
    bq = _dot(fp, pq_ref[...]) + oq_ref[...]
    gq = gq_ref[...]
    for h in range(FOX_H):
        sl = slice(h * HB, (h + 1) * HB)
        zh = zq[:, sl]
        ss = jnp.sum(zh * zh, axis=-1, keepdims=True)
        q_ref[:, sl] = (zh * lax.rsqrt(ss * (1.0 / FOX_HD) + EPS) * gq + bq[:, sl]).astype(BF16)
    zk = _dot(xb, wk_ref[...])
    bk = _dot(fp, pk_ref[...]) + ok_ref[...]
    gk = gk_ref[...]
    for h in range(FOX_HK):
        sl = slice(h * HB, (h + 1) * HB)
        zh = zk[:, sl]
        ss = jnp.sum(zh * zh, axis=-1, keepdims=True)
        kn = zh * lax.rsqrt(ss * (1.0 / FOX_HD) + EPS) * gk
        kf_ref[:, sl] = kn
        k_ref[:, sl] = (kn + bk[:, sl]).astype(BF16)
    zv = _dot(xb, wv_ref[...])
    vf_ref[...] = zv
    v_ref[...] = zv.astype(BF16)
    gate_ref[...] = jax.nn.sigmoid(_dot(xb, wg_ref[...])).astype(BF16)


def fox_proj(x, g_mix, wq, wk, wv, wg, gq_lay, gk_lay, fparts, place_q, place_k, ones_q, ones_k, tm):
    t = x.shape[0]
    row = lambda n: pl.BlockSpec((tm, n), lambda i: (i, 0))
    full = lambda a: pl.BlockSpec(a.shape, lambda i: (0,) * a.ndim)
    nq, nk = FOX_H * HB, FOX_HK * HB
    consts = (g_mix, wq, wk, wv, wg, gq_lay, gk_lay)
    tails = (place_q, place_k, ones_q, ones_k)
    return pl.pallas_call(
        _fox_proj_kernel, grid=(t // tm,),
        in_specs=[row(D_MODEL)] + [full(a) for a in consts] + [row(LANES)] + [full(a) for a in tails],
        out_specs=[row(nq), row(nk), row(nk), row(nk), row(nk), row(nq)],
        out_shape=[jax.ShapeDtypeStruct((t, nq), BF16), jax.ShapeDtypeStruct((t, nk), BF16),
                   jax.ShapeDtypeStruct((t, nk), F32), jax.ShapeDtypeStruct((t, nk), BF16),
                   jax.ShapeDtypeStruct((t, nk), F32), jax.ShapeDtypeStruct((t, nq), BF16)],
        compiler_params=_cparams(("parallel",)), name="fox_proj")(x, *consts, fparts, *tails)


def _fox_bias_tables():
    pq = np.zeros((LANES, FOX_H * HB), np.float32)
    oq = np.zeros((1, FOX_H * HB), np.float32)
    pk = np.zeros((LANES, FOX_HK * HB), np.float32)
    ok = np.zeros((1, FOX_HK * HB), np.float32)
    for h in range(FOX_H):
        kh, g = divmod(h, FOX_G)
        for part in range(3):
            pq[16 * part + h, h * HB + 64 + part] = 1.0
            oq[0, h * HB + 67 + 3 * g + part] = 1.0
            pk[16 * part + h, kh * HB + 67 + 3 * g + part] = -1.0
    for kh in range(FOX_HK):
        ok[0, kh * HB + 64: kh * HB + 67] = 1.0
    return (jnp.asarray(pq, BF16), jnp.asarray(pk, BF16), jnp.asarray(oq), jnp.asarray(ok))


def _head_mm_kernel(a_ref, w_ref, o_ref):
    o_ref[...] = _dot(a_ref[...], w_ref[0]).astype(o_ref.dtype)


def head_mm(a, w, out_dtype):
    n = a.shape[0]
    nh, ka, kb = w.shape
    return pl.pallas_call(
        _head_mm_kernel, grid=(nh,),
        in_specs=[pl.BlockSpec((n, ka), lambda h: (0, h)), pl.BlockSpec((1, ka, kb), lambda h: (h, 0, 0))],
        out_specs=pl.BlockSpec((n, kb), lambda h: (0, h)),
        out_shape=jax.ShapeDtypeStruct((n, nh * kb), out_dtype),
        compiler_params=_cparams(("parallel",)), name="head_mm")(a, w)


def _mla_dec_kernel(pt_ref, qa_ref, qp_ref, qn_ref, kn_ref, cn_ref, wukt_ref, gpe_ref, cos_ref, sin_ref,
                    *rest):
    ckv_refs = rest[:PG]
    kpe_refs = rest[PG:2 * PG]
    o_ref = rest[2 * PG]
    m_scr, l_scr, acc_scr = rest[2 * PG + 1:]
    p = pl.program_id(1)

    @pl.when(p == 0)
    def _():
        s_new = jnp.sum(qn_ref[0].astype(F32) * kn_ref[0].astype(F32), axis=-1, keepdims=True)
        m_scr[...] = jnp.broadcast_to(s_new, m_scr.shape)
        l_scr[...] = jnp.ones_like(l_scr)
        acc_scr[...] = jnp.broadcast_to(cn_ref[0], acc_scr.shape)

    qa = qa_ref[0]
    qp = qp_ref[0]
    wukt = wukt_ref[...]
    gpe = gpe_ref[...]
    eye = (lax.broadcasted_iota(jnp.int32, (MLA_ROPE, MLA_ROPE), 0)
           == lax.broadcasted_iota(jnp.int32, (MLA_ROPE, MLA_ROPE), 1)).astype(BF16)
    half = MLA_ROPE // 2
    for c in range(PG // 2):
        cb = jnp.concatenate([ckv_refs[2 * c][0, 0], ckv_refs[2 * c + 1][0, 0]], axis=0).astype(BF16)
        kp = jnp.concatenate([kpe_refs[2 * c][0, 0], kpe_refs[2 * c + 1][0, 0]], axis=0).astype(BF16)
        nk = cb.shape[0]
        knt = _dot_nt(wukt, cb)
        ss = jnp.sum((knt * knt).reshape(MLA_H, MLA_NOPE, nk), axis=1)
        kt = _dot_nt(eye, kp)
        ss = ss + jnp.sum(kt * kt, axis=0, keepdims=True)
        r = lax.rsqrt(ss * (1.0 / MLA_QK) + EPS)
        kg = kt * gpe
        cos = cos_ref[:, c * 2 * PAGE:(c + 1) * 2 * PAGE]
        sin = sin_ref[:, c * 2 * PAGE:(c + 1) * 2 * PAGE]
        k1, k2 = kg[:half], kg[half:]
        rk = jnp.concatenate([k1 * cos - k2 * sin, k1 * sin + k2 * cos], axis=0).astype(BF16)
        s = (_dot_nt(qa, cb) + _dot(qp, rk)) * r
        m_old = m_scr[...]
        m_new = jnp.maximum(m_old, jnp.max(s, axis=-1, keepdims=True))
        alpha = jnp.exp(m_old - m_new)
        pr = jnp.exp(s - m_new[:, :1])
        l_scr[...] = l_scr[...] * alpha + jnp.sum(pr, axis=-1, keepdims=True)
        acc_scr[...] = acc_scr[...] * alpha[:, :1] + _dot(pr.astype(BF16), cb)
        m_scr[...] = m_new

    @pl.when(p == pl.num_programs(1) - 1)
    def _():
        o_ref[0] = (acc_scr[...] / l_scr[:, :1]).astype(o_ref.dtype)


def mla_decode(page_table, layer, q_abs, q_pe, q_new, k_new, ckv_new, wukt, gpe_col, cos_t, sin_t,
               cache_ckv, cache_kpe):
    db, n_pages = page_table.shape
    steps = n_pages // PG
    per_b = lambda n: pl.BlockSpec((1, MLA_H, n), lambda b, p, pt: (b, 0, 0))
    full = lambda a: pl.BlockSpec(a.shape, lambda b, p, pt: (0,) * a.ndim)
    tab = pl.BlockSpec((MLA_ROPE // 2, PG * PAGE), lambda b, p, pt: (0, p))

    def page_spec(width, j):
        return pl.BlockSpec((1, 1, PAGE, width), lambda b, p, pt: (layer, pt[b, p * PG + j], 0, 0))

    grid_spec = pltpu.PrefetchScalarGridSpec(
        num_scalar_prefetch=1, grid=(db, steps),
        in_specs=[per_b(KV_LORA), per_b(MLA_ROPE), per_b(HB), per_b(HB),
                  pl.BlockSpec((1, 1, KV_LORA), lambda b, p, pt: (b, 0, 0)),
                  full(wukt), full(gpe_col), tab, tab]
                 + [page_spec(KV_LORA, j) for j in range(PG)]
                 + [page_spec(MLA_ROPE, j) for j in range(PG)],
        out_specs=per_b(KV_LORA),
        scratch_shapes=[pltpu.VMEM((MLA_H, LANES), F32), pltpu.VMEM((MLA_H, LANES), F32),
                        pltpu.VMEM((MLA_H, KV_LORA), F32)])
    return pl.pallas_call(
        _mla_dec_kernel, grid_spec=grid_spec,
        out_shape=jax.ShapeDtypeStruct((db, MLA_H, KV_LORA), BF16),
        compiler_params=_cparams(("parallel", "arbitrary")), name="mla_decode")(
            page_table, q_abs, q_pe, q_new, k_new, ckv_new, wukt, gpe_col, cos_t, sin_t,
            *([cache_ckv] * PG), *([cache_kpe] * PG))


def _fox_dec_kernel(pt_ref, q_ref, kn_ref, vn_ref, fq_ref, mask_ref, u_ref, *rest):
    k_refs = rest[:PG]
    v_refs = rest[PG:2 * PG]
    lf_refs = rest[2 * PG:3 * PG]
    o_ref = rest[3 * PG]
    m_scr, l_scr, acc_scr, carry = rest[3 * PG + 1:]
    p = pl.program_id(1)
    q = q_ref[0]

    @pl.when(p == 0)
    def _():
        s_new = jnp.sum(q.astype(F32) * kn_ref[0].astype(F32), axis=-1, keepdims=True)
        m_scr[...] = jnp.broadcast_to(s_new, m_scr.shape)
        l_scr[...] = jnp.ones_like(l_scr)
        acc_scr[...] = vn_ref[0]
        carry[...] = jnp.zeros_like(carry)

    mask = mask_ref[...] > 0.0
    fq = fq_ref[0]
    eye = (lax.broadcasted_iota(jnp.int32, (FOX_H, FOX_H), 0)
           == lax.broadcasted_iota(jnp.int32, (FOX_H, FOX_H), 1)).astype(BF16)
    u = u_ref[...]
    rows = PAGE * FOX_HK
    for j in range(PG - 1, -1, -1):
        kb = k_refs[j][0, 0].astype(BF16)
        vb = v_refs[j][0, 0].astype(BF16)
        lft = _dot_nt(eye, lf_refs[j][0, 0].astype(BF16))
        ext = _dot(lft.astype(BF16), u)
        bias = ext[:, :rows] + carry[:, :1] + fq[:, :1]
        carry[...] += ext[:, rows:]
        s = jnp.where(mask, _dot_nt(q, kb) + bias, -jnp.inf)
        m_old = m_scr[...]
        m_new = jnp.maximum(m_old, jnp.max(s, axis=-1, keepdims=True))
        alpha = jnp.exp(m_old - m_new)
        pr = jnp.exp(s - m_new[:, :1])
        l_scr[...] = l_scr[...] * alpha + jnp.sum(pr, axis=-1, keepdims=True)
        acc_scr[...] = acc_scr[...] * alpha[:, :FOX_HD] + _dot(pr.astype(BF16), vb)
        m_scr[...] = m_new

    @pl.when(p == pl.num_programs(1) - 1)
    def _():
        o_ref[0] = (acc_scr[...] / l_scr[:, :FOX_HD]).astype(o_ref.dtype)


def fox_decode(page_table, layer, q16, kn16, vn16, fq, cache_k, cache_v, cache_lf):
    db, n_pages = page_table.shape
    steps = n_pages // PG
    rows = PAGE * FOX_HK
    kv_of_row = np.arange(rows) % FOX_HK
    mask = (kv_of_row[None, :] == (np.arange(FOX_H) // FOX_G)[:, None]).astype(np.float32)
    u = np.zeros((PAGE, rows + LANES), np.float32)
    u[:, :rows] = (np.arange(PAGE)[:, None] > (np.arange(rows) // FOX_HK)[None, :])
    u[:, rows:] = 1.0
    mask, u = jnp.asarray(mask), jnp.asarray(u, BF16)
    per_b = lambda n: pl.BlockSpec((1, FOX_H, n), lambda b, p, pt: (b, 0, 0))
    full = lambda a: pl.BlockSpec(a.shape, lambda b, p, pt: (0,) * a.ndim)

    def page_spec(shape, j):
        nd = len(shape)
        return pl.BlockSpec((1, 1) + shape,
                            lambda b, p, pt: (layer, pt[b, (steps - 1 - p) * PG + j]) + (0,) * nd)

    kk = cache_k.reshape(cache_k.shape[0], cache_k.shape[1], rows, FOX_HD)
    vv = cache_v.reshape(cache_v.shape[0], cache_v.shape[1], rows, FOX_HD)
    grid_spec = pltpu.PrefetchScalarGridSpec(
        num_scalar_prefetch=1, grid=(db, steps),
        in_specs=[per_b(FOX_HD), per_b(FOX_HD), per_b(FOX_HD), per_b(LANES), full(mask), full(u)]
                 + [page_spec((rows, FOX_HD), j) for j in range(PG)]
                 + [page_spec((rows, FOX_HD), j) for j in range(PG)]
                 + [page_spec((PAGE, FOX_H), j) for j in range(PG)],
        out_specs=per_b(FOX_HD),
        scratch_shapes=[pltpu.VMEM((FOX_H, LANES), F32), pltpu.VMEM((FOX_H, LANES), F32),
                        pltpu.VMEM((FOX_H, FOX_HD), F32), pltpu.VMEM((FOX_H, LANES), F32)])
    return pl.pallas_call(
        _fox_dec_kernel, grid_spec=grid_spec,
        out_shape=jax.ShapeDtypeStruct((db, FOX_H, FOX_HD), F32),
        compiler_params=_cparams(("parallel", "arbitrary")), name="fox_decode")(
            page_table, q16, kn16, vn16, fq, mask, u,
            *([kk] * PG), *([vv] * PG), *([cache_lf] * PG))


def _mla_weights(w_in, w_qup, w_uk, w_uv, g_qn, g_kn, w_o):
    pe_lay = jnp.zeros((D_MODEL, HB), F32).at[:, MLA_ROPE_LANES].set(w_in[:, 640:672])
    w_ext = jnp.concatenate([w_in[:, :640], pe_lay, w_in[:, 640:672]], axis=1).astype(BF16)
    wq = _pad_heads(w_qup, MLA_H, MLA_QK, MLA_LANE).astype(BF16)
    nope_lane = MLA_LANE[:MLA_NOPE]
    wk = _pad_heads(w_uk, MLA_H, MLA_NOPE, nope_lane).astype(BF16)
    wv = _pad_heads(w_uv, MLA_H, MLA_V, np.arange(MLA_V)).astype(BF16)
    wo = _pad_heads(w_o.T, MLA_H, MLA_V, np.arange(MLA_V)).T.astype(BF16)
    gq_lay = _pad_vec(g_qn, MLA_LANE)
    gk_lay = _pad_vec(g_kn, MLA_LANE)
    wabs = w_uk.reshape(KV_LORA, MLA_H, MLA_NOPE).transpose(1, 2, 0) * g_kn[None, :MLA_NOPE, None]
    wabs = jnp.zeros((MLA_H, HB, KV_LORA), F32).at[:, nope_lane, :].set(wabs).astype(BF16)
    wuv_h = w_uv.reshape(KV_LORA, MLA_H, MLA_V).transpose(1, 0, 2)
    wuv_h = jnp.concatenate([wuv_h, jnp.zeros_like(wuv_h)], axis=-1).astype(BF16)
    wukt = w_uk.T.astype(BF16)
    gpe_col = g_kn[MLA_NOPE:].reshape(MLA_ROPE, 1)
    return dict(w_ext=w_ext, wq=wq, wk=wk, wv=wv, wo=wo, gq_lay=gq_lay, gk_lay=gk_lay, wabs=wabs,
                wuv_h=wuv_h, wukt=wukt, gpe_col=gpe_col)


def _fox_weights(w_in, b_f, g_qn, g_kn, w_o):
    nq, nk = FOX_H * FOX_HD, FOX_HK * FOX_HD
    ar = np.arange(FOX_HD)
    wq = (_pad_heads(w_in[:, :nq], FOX_H, FOX_HD, ar)).astype(BF16)
    wk = _pad_heads(w_in[:, nq:nq + nk], FOX_HK, FOX_HD, ar).astype(BF16)
    wv = _pad_heads(w_in[:, nq + nk:nq + 2 * nk], FOX_HK, FOX_HD, ar).astype(BF16)
    wg = _pad_heads(w_in[:, nq + 2 * nk:2 * nq + 2 * nk], FOX_H, FOX_HD, ar).astype(BF16)
    wf = w_in[:, 2 * nq + 2 * nk:]
    wf3 = jnp.concatenate([wf, wf, wf, jnp.zeros((D_MODEL, LANES - 3 * FOX_H), F32)], axis=1).astype(BF16)
    bf3 = jnp.concatenate([b_f, b_f, b_f, jnp.zeros((LANES - 3 * FOX_H,), F32)]).reshape(1, LANES)
    wo = _pad_heads(w_o.T, FOX_H, FOX_HD, ar).T.astype(BF16)
    gq_lay = _pad_vec(g_qn, ar) * FOX_SCALE
    gk_lay = _pad_vec(g_kn, ar)
    return dict(wq=wq, wk=wk, wv=wv, wg=wg, wf3=wf3, bf3=bf3, wo=wo, gq_lay=gq_lay, gk_lay=gk_lay)


def _unpad_heads(a, n_heads, hd):
    return a.reshape(a.shape[0], n_heads, HB)[:, :, :hd]


def kernel(x_prompt, x_sample, cache_mla_ckv, cache_mla_kpe, cache_fox_k, cache_fox_v, cache_fox_logf, page_table, norm_mix, norm_ffn, mla_w_in, mla_g_qlat, mla_g_kvlat, mla_w_qup, mla_w_uk, mla_w_uv, mla_g_qn, mla_g_kn, mla_w_o, fox_w_in, fox_b_f, fox_g_qn, fox_g_kn, fox_w_o, ffn_w_gu, ffn_w_down, moe_w_router, moe_w_gu, moe_w_down):
    bsz, seq, _ = x_prompt.shape
    db, tdec, _ = x_sample.shape
    assert tdec == 1
    n_pages = page_table.shape[1]
    past = n_pages * PAGE
    depth = norm_mix.shape[0]
    tp = 512 if seq % 512 == 0 else seq
    ts = db
    xp = x_prompt.reshape(bsz * seq, D_MODEL)
    xs = x_sample.reshape(db, D_MODEL)

    cp, sp = _rope_tables(jnp.arange(seq))
    cs, ss = _rope_tables(jnp.full((db,), past))
    half = MLA_ROPE // 2
    inv = 1.0 / (ROPE_THETA ** (jnp.arange(half, dtype=F32) / half))
    ang = inv[:, None] * jnp.arange(past, dtype=F32)[None, :]
    cos_t, sin_t = jnp.cos(ang), jnp.sin(ang)
    place_q, place_k, ones_q, ones_k = _fox_bias_tables()

    outs = {k: [] for k in ("p_ckv", "p_kpe", "s_ckv", "s_kpe", "p_fk", "p_fv", "p_flf", "s_fk", "s_fv", "s_flf")}
    for i in range(depth):
        j = i // 2
        g_mix = norm_mix[i].reshape(1, D_MODEL)
        g_ffn = norm_ffn[i].reshape(1, D_MODEL)
        if i % 2 == 0:
            w = _mla_weights(mla_w_in[j], mla_w_qup[j], mla_w_uk[j], mla_w_uv[j], mla_g_qn[j], mla_g_kn[j],
                             mla_w_o[j])
            gql, gkvl = mla_g_qlat[j].reshape(1, -1), mla_g_kvlat[j].reshape(1, -1)
            cq, ckv, kpl, kpe = mla_in(xp, g_mix, w["w_ext"], gql, gkvl, tp)
            q = mla_q(cq, w["wq"], w["gq_lay"], cp, sp, tp)
            k, v = mla_kv(ckv, kpl, w["wk"], w["wv"], w["gk_lay"], cp, sp, tp)
            o = flash_attention(q, k, v, bsz, seq, MLA_H, MLA_H, tp)
            xp = out_proj(xp, o, w["wo"], tp)
            outs["p_ckv"].append(ckv.reshape(bsz, seq, KV_LORA))
            outs["p_kpe"].append(kpe.reshape(bsz, seq, MLA_ROPE))
            cq, ckv, kpl, kpe = mla_in(xs, g_mix, w["w_ext"], gql, gkvl, ts)
            q = mla_q(cq, w["wq"], w["gq_lay"], cs, ss, ts)
            k, _ = mla_kv(ckv, kpl, w["wk"], w["wv"], w["gk_lay"], cs, ss, ts)
            q_abs = head_mm(q, w["wabs"], BF16).reshape(db, MLA_H, KV_LORA)
            q3 = q.reshape(db, MLA_H, HB)
            q_pe = q3[:, :, MLA_ROPE_LANES]
            o_lat = mla_decode(page_table, j, q_abs, q_pe, q3, k.reshape(db, MLA_H, HB),
                               ckv.reshape(db, 1, KV_LORA), w["wukt"], w["gpe_col"], cos_t, sin_t,
                               cache_mla_ckv, cache_mla_kpe)
            o = head_mm(o_lat.reshape(db, MLA_H * KV_LORA), w["wuv_h"], BF16)
            xs = out_proj(xs, o, w["wo"], ts)
            outs["s_ckv"].append(ckv.reshape(db, 1, KV_LORA))
            outs["s_kpe"].append(kpe.reshape(db, 1, MLA_ROPE))
        else:
            w = _fox_weights(fox_w_in[j], fox_b_f[j], fox_g_qn[j], fox_g_kn[j], fox_w_o[j])
            lf, fparts = fox_lf(xp, g_mix, w["wf3"], w["bf3"], bsz, seq, tp, True)
            q, k, kf, v, vf, gate = fox_proj(xp, g_mix, w["wq"], w["wk"], w["wv"], w["wg"], w["gq_lay"],
                                             w["gk_lay"], fparts, place_q, place_k, ones_q, ones_k, tp)
            o = flash_attention(q, k, v, bsz, seq, FOX_H, FOX_HK, tp)
            xp = out_proj(xp, o, w["wo"], tp, gate=gate)
            outs["p_fk"].append(_unpad_heads(kf, FOX_HK, FOX_HD).reshape(bsz, seq, FOX_HK, FOX_HD))
            outs["p_fv"].append(_unpad_heads(vf, FOX_HK, FOX_HD).reshape(bsz, seq, FOX_HK, FOX_HD))
            outs["p_flf"].append(lf[:, :FOX_H].reshape(bsz, seq, FOX_H))
            lf, fparts = fox_lf(xs, g_mix, w["wf3"], w["bf3"], 1, db, ts, False)
            q, k, kf, v, vf, gate = fox_proj(xs, g_mix, w["wq"], w["wk"], w["wv"], w["wg"], w["gq_lay"],
                                             w["gk_lay"], fparts, place_q, place_k, ones_q, ones_k, ts)
            q16 = _unpad_heads(q, FOX_H, FOX_HD)
            kf4 = _unpad_heads(kf, FOX_HK, FOX_HD)
            vf4 = _unpad_heads(vf, FOX_HK, FOX_HD)
            kn16 = jnp.repeat(kf4, FOX_G, axis=1)
            vn16 = jnp.repeat(vf4, FOX_G, axis=1)
            fq = jnp.broadcast_to(lf[:, :FOX_H, None], (db, FOX_H, LANES))
            o16 = fox_decode(page_table, j, q16, kn16, vn16, fq, cache_fox_k, cache_fox_v, cache_fox_logf)
            o = jnp.concatenate([o16, jnp.zeros_like(o16)], axis=-1).reshape(db, FOX_H * HB).astype(BF16)
            xs = out_proj(xs, o, w["wo"], ts, gate=gate)
            outs["s_fk"].append(kf4.reshape(db, 1, FOX_HK, FOX_HD))
            outs["s_fv"].append(vf4.reshape(db, 1, FOX_HK, FOX_HD))
            outs["s_flf"].append(lf[:, :FOX_H].reshape(db, 1, FOX_H))
        if i % 2 == 0:
            wgu, wdn = ffn_w_gu[j].astype(BF16), ffn_w_down[j].astype(BF16)
            fc = wdn.shape[0] // 2
            xp = ffn_dense(xp, g_ffn, wgu, wdn, tp, fc)
            xs = ffn_dense(xs, g_ffn, wgu, wdn, ts, fc)
        else:
            wgu, wdn = moe_w_gu[j].astype(BF16), moe_w_down[j].astype(BF16)
            wr = jnp.concatenate([moe_w_router[j], jnp.zeros((D_MODEL, LANES - N_EXP), F32)], axis=1)
            xp = moe_layer(xp, g_ffn, wr, wgu, wdn, tp, tp, 512)
            xs = moe_layer(xs, g_ffn, wr, wgu, wdn, ts, ts, 512)
    st = lambda name: jnp.stack(outs[name])
    return (xp.reshape(bsz, seq, D_MODEL), xs.reshape(db, 1, D_MODEL),
            st("p_ckv"), st("p_kpe"), st("p_fk"), st("p_fv"), st("p_flf"),
            st("s_ckv"), st("s_kpe"), st("s_fk"), st("s_fv"), st("s_flf"))
```

```python
import functools

import numpy as np
import jax
import jax.numpy as jnp
from jax import lax
from jax.experimental import pallas as pl
from jax.experimental.pallas import tpu as pltpu

F32 = jnp.float32
BF16 = jnp.bfloat16

D_MODEL = 1024
EPS = 1e-6
ROPE_THETA = 10000.0
PAGE = 128
MLA_H, MLA_NOPE, MLA_ROPE, MLA_V = 16, 64, 32, 64
MLA_QK = MLA_NOPE + MLA_ROPE
Q_LORA, KV_LORA = 384, 256
MLA_SCALE = MLA_QK ** -0.5
FOX_H, FOX_HK, FOX_HD = 16, 4, 64
FOX_G = FOX_H // FOX_HK
FOX_SCALE = FOX_HD ** -0.5
N_EXP = 8

LANES = 128
VMEM_LIMIT = 56 * 1024 * 1024
HB = LANES
PG = 16


def _cparams(sem):
    return pltpu.CompilerParams(dimension_semantics=sem, vmem_limit_bytes=VMEM_LIMIT)


def _rms(x, g):
    return x * lax.rsqrt(jnp.mean(x * x, axis=-1, keepdims=True) + EPS) * g


def _dot(a, b):
    return jnp.dot(a, b, preferred_element_type=F32)


def _dot_nt(a, b):
    return lax.dot_general(a, b, (((1,), (1,)), ((), ())), preferred_element_type=F32)


def _dot_tn(a, b):
    return lax.dot_general(a, b, (((0,), (0,)), ((), ())), preferred_element_type=F32)


def _mla_lane_of_dim():
    lane = np.zeros(MLA_QK, np.int32)
    for d in range(MLA_QK):
        if d < 32:
            lane[d] = d
        elif d < 64:
            lane[d] = 64 + (d - 32)
        elif d < 80:
            lane[d] = 32 + (d - 64)
        else:
            lane[d] = 96 + (d - 80)
    return lane


MLA_LANE = _mla_lane_of_dim()
MLA_ROPE_LANES = np.concatenate([32 + np.arange(16), 96 + np.arange(16)])


def _inv_index(lane_of_dim, n_dims):
    inv = np.full(HB, n_dims, np.int32)
    inv[lane_of_dim] = np.arange(len(lane_of_dim))
    return inv


def _pad_heads(w, n_heads, hd, lane_of_dim):
    k = w.shape[0]
    w3 = w.reshape(k, n_heads, hd)
    w3 = jnp.concatenate([w3, jnp.zeros((k, n_heads, 1), w.dtype)], axis=-1)
    return w3[:, :, _inv_index(lane_of_dim, hd)].reshape(k, n_heads * HB)


def _pad_vec(g, lane_of_dim):
    g1 = jnp.concatenate([g, jnp.zeros((1,), g.dtype)])
    return g1[_inv_index(lane_of_dim, g.shape[0])].reshape(1, HB)


def _rope_tables(pos):
    half = MLA_ROPE // 2
    inv = 1.0 / (ROPE_THETA ** (jnp.arange(half, dtype=F32) / half))
    ang = pos.astype(F32)[:, None] * inv
    cos, sin = jnp.cos(ang), jnp.sin(ang)
    n = pos.shape[0]
    c = jnp.ones((n, HB), F32).at[:, 32:48].set(cos).at[:, 96:112].set(cos)
    s = jnp.zeros((n, HB), F32).at[:, 32:48].set(-sin).at[:, 96:112].set(sin)
    return c, s


def _mla_in_kernel(x_ref, g_ref, w_ref, gq_ref, gkv_ref, cq_ref, ckv_ref, kpl_ref, kpe_ref):
    xn = _rms(x_ref[...], g_ref[...])
    z = _dot(xn.astype(BF16), w_ref[...])
    cq_ref[...] = _rms(z[:, :Q_LORA], gq_ref[...]).astype(BF16)
    ckv_ref[...] = _rms(z[:, Q_LORA:Q_LORA + KV_LORA], gkv_ref[...])
    kpl_ref[...] = z[:, 640:768]
    kpe_ref[...] = z[:, 768:800]


def mla_in(x, g_mix, w_ext, g_qlat, g_kvlat, tm):
    t = x.shape[0]
    row = lambda n: pl.BlockSpec((tm, n), lambda i: (i, 0))
    full = lambda a: pl.BlockSpec(a.shape, lambda i: (0,) * a.ndim)
    return pl.pallas_call(
        _mla_in_kernel, grid=(t // tm,),
        in_specs=[row(D_MODEL), full(g_mix), full(w_ext), full(g_qlat), full(g_kvlat)],
        out_specs=[row(Q_LORA), row(KV_LORA), row(HB), row(MLA_ROPE)],
        out_shape=[jax.ShapeDtypeStruct((t, Q_LORA), BF16), jax.ShapeDtypeStruct((t, KV_LORA), F32),
                   jax.ShapeDtypeStruct((t, HB), F32), jax.ShapeDtypeStruct((t, MLA_ROPE), F32)],
        compiler_params=_cparams(("parallel",)), name="mla_in")(x, g_mix, w_ext, g_qlat, g_kvlat)


def _head_norm_rope(zh, g, c, s, dim, scale):
    ss = jnp.sum(zh * zh, axis=-1, keepdims=True)
    zn = zh * lax.rsqrt(ss * (1.0 / dim) + EPS) * g
    out = zn * c + pltpu.roll(zn, HB // 2, 1) * s
    return out * scale if scale != 1.0 else out


def _mla_q_kernel(cq_ref, w_ref, g_ref, c_ref, s_ref, q_ref):
    z = _dot(cq_ref[...], w_ref[...])
    g, c, s = g_ref[...], c_ref[...], s_ref[...]
    for h in range(MLA_H):
        sl = slice(h * HB, (h + 1) * HB)
        q_ref[:, sl] = _head_norm_rope(z[:, sl], g, c, s, MLA_QK, MLA_SCALE).astype(BF16)


def mla_q(cq, wq, gq_lay, ctab, stab, tm):
    t = cq.shape[0]
    nb = ctab.shape[0] // tm
    row = lambda n: pl.BlockSpec((tm, n), lambda i: (i, 0))
    tab = pl.BlockSpec((tm, HB), lambda i: (i % nb, 0))
    full = lambda a: pl.BlockSpec(a.shape, lambda i: (0,) * a.ndim)
    return pl.pallas_call(
        _mla_q_kernel, grid=(t // tm,),
        in_specs=[row(Q_LORA), full(wq), full(gq_lay), tab, tab],
        out_specs=row(MLA_H * HB),
        out_shape=jax.ShapeDtypeStruct((t, MLA_H * HB), BF16),
        compiler_params=_cparams(("parallel",)), name="mla_q")(cq, wq, gq_lay, ctab, stab)


def _mla_kv_kernel(ckv_ref, kpl_ref, wk_ref, wv_ref, g_ref, c_ref, s_ref, k_ref, v_ref):
    cb = ckv_ref[...].astype(BF16)
    kn = _dot(cb, wk_ref[...])
    v_ref[...] = _dot(cb, wv_ref[...]).astype(BF16)
    kpl, g, c, s = kpl_ref[...], g_ref[...], c_ref[...], s_ref[...]
    for h in range(MLA_H):
        sl = slice(h * HB, (h + 1) * HB)
        k_ref[:, sl] = _head_norm_rope(kn[:, sl] + kpl, g, c, s, MLA_QK, 1.0).astype(BF16)


def mla_kv(ckv, kpl, wk, wv, gk_lay, ctab, stab, tm):
    t = ckv.shape[0]
    nb = ctab.shape[0] // tm
    row = lambda n: pl.BlockSpec((tm, n), lambda i: (i, 0))
    tab = pl.BlockSpec((tm, HB), lambda i: (i % nb, 0))
    full = lambda a: pl.BlockSpec(a.shape, lambda i: (0,) * a.ndim)
    return pl.pallas_call(
        _mla_kv_kernel, grid=(t // tm,),
        in_specs=[row(KV_LORA), row(HB), full(wk), full(wv), full(gk_lay), tab, tab],
        out_specs=[row(MLA_H * HB), row(MLA_H * HB)],
        out_shape=[jax.ShapeDtypeStruct((t, MLA_H * HB), BF16)] * 2,
        compiler_params=_cparams(("parallel",)), name="mla_kv")(ckv, kpl, wk, wv, gk_lay, ctab, stab)


def _flash_kernel(q_ref, k_ref, v_ref, o_ref, *, tq):
    qi = pl.program_id(2)
    q = q_ref[...]

    def step(j, carry, masked):
        m, l, acc = carry
        start = pl.multiple_of(j * tq, tq)
        kj = k_ref[pl.ds(start, tq), :]
        vj = v_ref[pl.ds(start, tq), :]
        s = _dot_nt(q, kj)
        if masked:
            r = lax.broadcasted_iota(jnp.int32, (tq, tq), 0)
            c = lax.broadcasted_iota(jnp.int32, (tq, tq), 1)
            s = jnp.where(c <= r, s, -jnp.inf)
        m_new = jnp.maximum(m, jnp.max(s, axis=-1, keepdims=True))
        alpha = jnp.exp(m - m_new)
        p = jnp.exp(s - m_new)
        l = l * alpha + jnp.sum(p, axis=-1, keepdims=True)
        acc = acc * alpha + _dot(p.astype(BF16), vj)
        return m_new, l, acc

    init = (jnp.full((tq, 1), -jnp.inf, F32), jnp.zeros((tq, 1), F32), jnp.zeros((tq, HB), F32))
    carry = lax.fori_loop(0, qi, lambda j, c: step(j, c, False), init)
    m, l, acc = step(qi, carry, True)
    o_ref[...] = (acc / l).astype(BF16)


def flash_attention(q, k, v, batch, seq, n_heads, n_kv, tq):
    nq = seq // tq
    grp = n_heads // n_kv
    return pl.pallas_call(
        functools.partial(_flash_kernel, tq=tq), grid=(batch, n_heads, nq),
        in_specs=[pl.BlockSpec((tq, HB), lambda b, h, i: (b * nq + i, h)),
                  pl.BlockSpec((seq, HB), lambda b, h, i: (b, h // grp)),
                  pl.BlockSpec((seq, HB), lambda b, h, i: (b, h // grp))],
        out_specs=pl.BlockSpec((tq, HB), lambda b, h, i: (b * nq + i, h)),
        out_shape=jax.ShapeDtypeStruct((batch * seq, n_heads * HB), BF16),
        compiler_params=_cparams(("parallel", "parallel", "arbitrary")), name="flash")(q, k, v)


def _out_proj_kernel(x_ref, o_ref, w_ref, y_ref):
    y_ref[...] = x_ref[...] + _dot(o_ref[...], w_ref[...])


def _out_proj_gate_kernel(x_ref, o_ref, g_ref, w_ref, y_ref):
    og = (o_ref[...].astype(F32) * g_ref[...].astype(F32)).astype(BF16)
    y_ref[...] = x_ref[...] + _dot(og, w_ref[...])


def out_proj(x, o, w, tm, gate=None):
    t = x.shape[0]
    n = o.shape[1]
    row = lambda c: pl.BlockSpec((tm, c), lambda i: (i, 0))
    wspec = pl.BlockSpec(w.shape, lambda i: (0, 0))
    if gate is None:
        kern, args, specs = _out_proj_kernel, (x, o, w), [row(D_MODEL), row(n), wspec]
    else:
        kern, args, specs = _out_proj_gate_kernel, (x, o, gate, w), [row(D_MODEL), row(n), row(n), wspec]
    return pl.pallas_call(
        kern, grid=(t // tm,), in_specs=specs, out_specs=row(D_MODEL),
        out_shape=jax.ShapeDtypeStruct((t, D_MODEL), F32),
        compiler_params=_cparams(("parallel",)), name="out_proj")(*args)


def _swiglu_chunk(hb, wg, wu, wd):
    g = _dot(hb, wg)
    u = _dot(hb, wu)
    a = (g * jax.nn.sigmoid(g) * u).astype(BF16)
    return _dot(a, wd)


def _ffn_kernel(x_ref, g_ref, wg_ref, wu_ref, wd_ref, y_ref, h_scr, acc_scr):
    c = pl.program_id(1)

    @pl.when(c == 0)
    def _():
        x = x_ref[...]
        h_scr[...] = _rms(x, g_ref[...]).astype(BF16)
        acc_scr[...] = x

    acc_scr[...] += _swiglu_chunk(h_scr[...], wg_ref[...], wu_ref[...], wd_ref[...])

    @pl.when(c == pl.num_programs(1) - 1)
    def _():
        y_ref[...] = acc_scr[...]


def ffn_dense(x, g, w_gu, w_down, tm, fc):
    t = x.shape[0]
    dff = w_down.shape[0]
    nc = dff // fc
    return pl.pallas_call(
        _ffn_kernel, grid=(t // tm, nc),
        in_specs=[pl.BlockSpec((tm, D_MODEL), lambda i, c: (i, 0)),
                  pl.BlockSpec((1, D_MODEL), lambda i, c: (0, 0)),
                  pl.BlockSpec((D_MODEL, fc), lambda i, c: (0, c)),
                  pl.BlockSpec((D_MODEL, fc), lambda i, c: (0, nc + c)),
                  pl.BlockSpec((fc, D_MODEL), lambda i, c: (c, 0))],
        out_specs=pl.BlockSpec((tm, D_MODEL), lambda i, c: (i, 0)),
        out_shape=jax.ShapeDtypeStruct((t, D_MODEL), F32),
        scratch_shapes=[pltpu.VMEM((tm, D_MODEL), BF16), pltpu.VMEM((tm, D_MODEL), F32)],
        compiler_params=_cparams(("parallel", "arbitrary")), name="ffn_dense")(x, g, w_gu, w_gu, w_down)


def _expert_kernel(te_ref, tv_ref, xs_ref, wg_ref, wu_ref, wd_ref, ys_ref, h_scr, acc_scr):
    i = pl.program_id(0)
    c = pl.program_id(1)

    @pl.when(tv_ref[i] > 0)
    def _():
        @pl.when(c == 0)
        def _():
            h_scr[...] = xs_ref[...].astype(BF16)
            acc_scr[...] = jnp.zeros_like(acc_scr)

        acc_scr[...] += _swiglu_chunk(h_scr[...], wg_ref[0], wu_ref[0], wd_ref[0])

        @pl.when(c == pl.num_programs(1) - 1)
        def _():
            ys_ref[...] = acc_scr[...]

    @pl.when((tv_ref[i] == 0) & (c == 0))
    def _():
        ys_ref[...] = jnp.zeros_like(ys_ref)


def moe_experts(tile_expert, tile_valid, xs, w_gu, w_down, tme, fc):
    npad = xs.shape[0]
    dff = w_down.shape[1]
    nc = dff // fc
    last = nc - 1
    cc = lambda i, c, tv: jnp.where(tv[i] > 0, c, last)
    grid_spec = pltpu.PrefetchScalarGridSpec(
        num_scalar_prefetch=2, grid=(npad // tme, nc),
        in_specs=[pl.BlockSpec((tme, D_MODEL), lambda i, c, te, tv: (i, 0)),
                  pl.BlockSpec((1, D_MODEL, fc), lambda i, c, te, tv: (te[i], 0, cc(i, c, tv))),
                  pl.BlockSpec((1, D_MODEL, fc), lambda i, c, te, tv: (te[i], 0, nc + cc(i, c, tv))),
                  pl.BlockSpec((1, fc, D_MODEL), lambda i, c, te, tv: (te[i], cc(i, c, tv), 0))],
        out_specs=pl.BlockSpec((tme, D_MODEL), lambda i, c, te, tv: (i, 0)),
        scratch_shapes=[pltpu.VMEM((tme, D_MODEL), BF16), pltpu.VMEM((tme, D_MODEL), F32)])
    return pl.pallas_call(
        _expert_kernel, grid_spec=grid_spec,
        out_shape=jax.ShapeDtypeStruct((npad, D_MODEL), F32),
        compiler_params=_cparams(("arbitrary", "arbitrary")), name="moe_experts")(
            tile_expert, tile_valid, xs, w_gu, w_gu, w_down)


def _router_kernel(x_ref, g_ref, wr_ref, tri_ref, h_ref, meta_ref, cnt_ref, carry):
    i = pl.program_id(0)

    @pl.when(i == 0)
    def _():
        carry[...] = jnp.zeros_like(carry)

    h = _rms(x_ref[...], g_ref[...])
    h_ref[...] = h
    logits = jnp.dot(h, wr_ref[...], preferred_element_type=F32, precision=lax.Precision.HIGHEST)
    tm = logits.shape[0]
    lane = lax.broadcasted_iota(jnp.int32, (tm, LANES), 1).astype(F32)
    logits = jnp.where(lane < N_EXP, logits, -jnp.inf)
    m1 = jnp.max(logits, axis=-1, keepdims=True)
    i1 = jnp.min(jnp.where(logits == m1, lane, float(LANES)), axis=-1, keepdims=True)
    rest = jnp.where(lane == i1, -jnp.inf, logits)
    m2 = jnp.max(rest, axis=-1, keepdims=True)
    i2 = jnp.min(jnp.where(rest == m2, lane, float(LANES)), axis=-1, keepdims=True)
    e = jnp.exp(m2 - m1)
    w1 = 1.0 / (1.0 + e)
    w2 = e * w1
    hot = ((lane == i1) | (lane == i2)).astype(F32)
    before = _dot(tri_ref[...], hot.astype(BF16)) + carry[...]
    r1 = jnp.sum(jnp.where(lane == i1, before, 0.0), axis=-1, keepdims=True)
    r2 = jnp.sum(jnp.where(lane == i2, before, 0.0), axis=-1, keepdims=True)
    carry[...] += jnp.sum(hot, axis=0, keepdims=True)
    cnt_ref[...] = jnp.broadcast_to(carry[...], cnt_ref.shape)
    meta = jnp.where(lane == 0, i1, 0.0)
    meta = jnp.where(lane == 1, i2, meta)
    meta = jnp.where(lane == 2, w1, meta)
    meta = jnp.where(lane == 3, w2, meta)
    meta = jnp.where(lane == 4, r1, meta)
    meta = jnp.where(lane == 5, r2, meta)
    meta_ref[...] = meta


def moe_router(x, g, wr_pad, tm):
    t = x.shape[0]
    tri = jnp.asarray(np.tril(np.ones((tm, tm), np.float32), -1), BF16)
    return pl.pallas_call(
        _router_kernel, grid=(t // tm,),
        in_specs=[pl.BlockSpec((tm, D_MODEL), lambda i: (i, 0)),
                  pl.BlockSpec((1, D_MODEL), lambda i: (0, 0)),
                  pl.BlockSpec((D_MODEL, LANES), lambda i: (0, 0)),
                  pl.BlockSpec((tm, tm), lambda i: (0, 0))],
        out_specs=[pl.BlockSpec((tm, D_MODEL), lambda i: (i, 0)),
                   pl.BlockSpec((tm, LANES), lambda i: (i, 0)),
                   pl.BlockSpec((8, LANES), lambda i: (0, 0))],
        out_shape=[jax.ShapeDtypeStruct((t, D_MODEL), F32), jax.ShapeDtypeStruct((t, LANES), F32),
                   jax.ShapeDtypeStruct((8, LANES), F32)],
        scratch_shapes=[pltpu.VMEM((1, LANES), F32)],
        compiler_params=_cparams(("arbitrary",)), name="moe_router")(x, g, wr_pad, tri)


def _row_copy(src_hbm, dst_ref, src_row, dst_row, sem):
    return pltpu.make_async_copy(src_hbm.at[pl.ds(src_row, 1)], dst_ref.at[pl.ds(dst_row, 1)], sem)


def _dispatch_kernel(src_ref, h_hbm, xs_ref, sem, *, rows):
    def issue(r, _):
        _row_copy(h_hbm, xs_ref, src_ref[0, 0, r], r, sem).start()
        return 0

    lax.fori_loop(0, rows, issue, 0, unroll=8)

    def drain(r, _):
        _row_copy(h_hbm, xs_ref, 0, r, sem).wait()
        return 0

    lax.fori_loop(0, rows, drain, 0, unroll=8)


def moe_dispatch(src, h, rows):
    npad = src.shape[0]
    src3 = src.reshape(npad // rows, 1, rows)
    return pl.pallas_call(
        functools.partial(_dispatch_kernel, rows=rows), grid=(npad // rows,),
        in_specs=[pl.BlockSpec((1, 1, rows), lambda i: (i, 0, 0), memory_space=pltpu.SMEM),
                  pl.BlockSpec(memory_space=pl.ANY)],
        out_specs=pl.BlockSpec((rows, D_MODEL), lambda i: (i, 0)),
        out_shape=jax.ShapeDtypeStruct((npad, D_MODEL), F32),
        scratch_shapes=[pltpu.SemaphoreType.DMA(())],
        compiler_params=_cparams(("arbitrary",)), name="moe_dispatch")(src3, h)


def _combine_kernel(pos_ref, x_ref, meta_ref, ys_hbm, y_ref, buf, sem, *, rows):
    def issue(r, _):
        _row_copy(ys_hbm, buf.at[0], pos_ref[0, 0, r], r, sem).start()
        _row_copy(ys_hbm, buf.at[1], pos_ref[0, 1, r], r, sem).start()
        return 0

    lax.fori_loop(0, rows, issue, 0, unroll=8)

    def drain(r, _):
        _row_copy(ys_hbm, buf.at[0], 0, r, sem).wait()
        _row_copy(ys_hbm, buf.at[1], 0, r, sem).wait()
        return 0

    lax.fori_loop(0, rows, drain, 0, unroll=8)
    meta = meta_ref[...]
    y_ref[...] = x_ref[...] + meta[:, 2:3] * buf[0] + meta[:, 3:4] * buf[1]


def moe_combine(pos, x, meta, ys, rows):
    t = x.shape[0]
    pos3 = pos.T.reshape(2, t // rows, rows).transpose(1, 0, 2)
    return pl.pallas_call(
        functools.partial(_combine_kernel, rows=rows), grid=(t // rows,),
        in_specs=[pl.BlockSpec((1, 2, rows), lambda i: (i, 0, 0), memory_space=pltpu.SMEM),
                  pl.BlockSpec((rows, D_MODEL), lambda i: (i, 0)),
                  pl.BlockSpec((rows, LANES), lambda i: (i, 0)),
                  pl.BlockSpec(memory_space=pl.ANY)],
        out_specs=pl.BlockSpec((rows, D_MODEL), lambda i: (i, 0)),
        out_shape=jax.ShapeDtypeStruct((t, D_MODEL), F32),
        scratch_shapes=[pltpu.VMEM((2, rows, D_MODEL), F32), pltpu.SemaphoreType.DMA(())],
        compiler_params=_cparams(("arbitrary",)), name="moe_combine")(pos3, x, meta, ys)


def moe_layer(x, g, wr_pad, w_gu, w_down, tm, tme, fc):
    t = x.shape[0]
    h, meta, cnt = moe_router(x, g, wr_pad, tm)
    ids = meta[:, 0:2].astype(jnp.int32)
    rank = meta[:, 4:6].astype(jnp.int32)
    counts = cnt[0, :N_EXP].astype(jnp.int32)
    padded = ((counts + tme - 1) // tme) * tme
    ends = jnp.cumsum(padded)
    offs = ends - padded
    pos = offs[ids] + rank
    n_tiles = (2 * t) // tme + N_EXP
    npad = n_tiles * tme
    tile_start = jnp.arange(n_tiles, dtype=jnp.int32) * tme
    tile_valid = (tile_start < ends[-1]).astype(jnp.int32)
    last_start = jnp.maximum(ends[-1] - tme, 0)
    clamped = jnp.minimum(tile_start, last_start)
    tile_expert = jnp.sum((clamped[:, None] >= ends[None, :]).astype(jnp.int32), axis=1)
    tile_expert = jnp.minimum(tile_expert, N_EXP - 1)
    tok = jnp.repeat(jnp.arange(t, dtype=jnp.int32), 2)
    src = jnp.zeros((npad,), jnp.int32).at[pos.reshape(-1)].set(tok)
    xs = moe_dispatch(src, h, tme)
    ys = moe_experts(tile_expert, tile_valid, xs, w_gu, w_down, tme, fc)
    return moe_combine(pos, x, meta, ys, min(256, t))


def _fox_lf_kernel(x_ref, g_ref, w_ref, b_ref, tri_ref, lf_ref, fp_ref, carry, *, cumulative):
    xn = _rms(x_ref[...], g_ref[...])
    z = _dot(xn.astype(BF16), w_ref[...]) + b_ref[...]
    lf = jnp.minimum(z, 0.0) - jnp.log1p(jnp.exp(-jnp.abs(z)))
    lf_ref[...] = lf
    if cumulative:
        @pl.when(pl.program_id(1) == 0)
        def _():
            carry[...] = jnp.zeros_like(carry)

        hi = lf.astype(BF16)
        r1 = lf - hi.astype(F32)
        mid = r1.astype(BF16)
        lo = (r1 - mid.astype(F32)).astype(BF16)
        tri = tri_ref[...]
        f = _dot(tri, hi) + _dot(tri, mid) + _dot(tri, lo) + carry[...]
        carry[...] = f[-1:, :]
    else:
        f = lf
    hi = f.astype(BF16)
    r1 = f - hi.astype(F32)
    mid = r1.astype(BF16)
    lo = (r1 - mid.astype(F32)).astype(BF16)
    lane = lax.broadcasted_iota(jnp.int32, f.shape, 1)
    fp_ref[...] = jnp.where(lane < 16, hi, jnp.where(lane < 32, mid, lo))


def fox_lf(x, g_mix, wf3, bf3, batch, seq, tm, cumulative):
    ns = seq // tm
    tri = jnp.asarray(np.tril(np.ones((tm, tm), np.float32)), BF16)
    row = lambda n: pl.BlockSpec((tm, n), lambda b, i: (b * ns + i, 0))
    full = lambda a: pl.BlockSpec(a.shape, lambda b, i: (0,) * a.ndim)
    t = batch * seq
    return pl.pallas_call(
        functools.partial(_fox_lf_kernel, cumulative=cumulative), grid=(batch, ns),
        in_specs=[row(D_MODEL), full(g_mix), full(wf3), full(bf3), full(tri)],
        out_specs=[row(LANES), row(LANES)],
        out_shape=[jax.ShapeDtypeStruct((t, LANES), F32), jax.ShapeDtypeStruct((t, LANES), BF16)],
        scratch_shapes=[pltpu.VMEM((1, LANES), F32)],
        compiler_params=_cparams(("arbitrary", "arbitrary")), name="fox_lf")(x, g_mix, wf3, bf3, tri)


def _fox_proj_kernel(x_ref, g_ref, wq_ref, wk_ref, wv_ref, wg_ref, gq_ref, gk_ref, fp_ref,
                     pq_ref, pk_ref, oq_ref, ok_ref,
                     q_ref, k_ref, kf_ref, v_ref, vf_ref, gate_ref):
    xb = _rms(x_ref[...], g_ref[...]).astype(BF16)
    fp = fp_ref[...]
    zq = _dot(xb, wq_ref[...])
    bq = _dot(fp, pq_ref[...]) + oq_ref[...]
    gq = gq_ref[...]
    for h in range(FOX_H):
        sl = slice(h * HB, (h + 1) * HB)
        zh = zq[:, sl]
        ss = jnp.sum(zh * zh, axis=-1, keepdims=True)
        q_ref[:, sl] = (zh * lax.rsqrt(ss * (1.0 / FOX_HD) + EPS) * gq + bq[:, sl]).astype(BF16)
    zk = _dot(xb, wk_ref[...])
    bk = _dot(fp, pk_ref[...]) + ok_ref[...]
    gk = gk_ref[...]
    for h in range(FOX_HK):
        sl = slice(h * HB, (h + 1) * HB)
        zh = zk[:, sl]
        ss = jnp.sum(zh * zh, axis=-1, keepdims=True)
        kn = zh * lax.rsqrt(ss * (1.0 / FOX_HD) + EPS) * gk
        kf_ref[:, sl] = kn
        k_ref[:, sl] = (kn + bk[:, sl]).astype(BF16)
    zv = _dot(xb, wv_ref[...])
    vf_ref[...] = zv
    v_ref[...] = zv.astype(BF16)
    gate_ref[...] = jax.nn.sigmoid(_dot(xb, wg_ref[...])).astype(BF16)


def fox_proj(x, g_mix, wq, wk, wv, wg, gq_lay, gk_lay, fparts, place_q, place_k, ones_q, ones_k, tm):
    t = x.shape[0]
    row = lambda n: pl.BlockSpec((tm, n), lambda i: (i, 0))
    full = lambda a: pl.BlockSpec(a.shape, lambda i: (0,) * a.ndim)
    nq, nk = FOX_H * HB, FOX_HK * HB
    consts = (g_mix, wq, wk, wv, wg, gq_lay, gk_lay)
    tails = (place_q, place_k, ones_q, ones_k)
    return pl.pallas_call(
        _fox_proj_kernel, grid=(t // tm,),
        in_specs=[row(D_MODEL)] + [full(a) for a in consts] + [row(LANES)] + [full(a) for a in tails],
        out_specs=[row(nq), row(nk), row(nk), row(nk), row(nk), row(nq)],
        out_shape=[jax.ShapeDtypeStruct((t, nq), BF16), jax.ShapeDtypeStruct((t, nk), BF16),
                   jax.ShapeDtypeStruct((t, nk), F32), jax.ShapeDtypeStruct((t, nk), BF16),
                   jax.ShapeDtypeStruct((t, nk), F32), jax.ShapeDtypeStruct((t, nq), BF16)],
        compiler_params=_cparams(("parallel",)), name="fox_proj")(x, *consts, fparts, *tails)


def _fox_bias_tables():
    pq = np.zeros((LANES, FOX_H * HB), np.float32)
    oq = np.zeros((1, FOX_H * HB), np.float32)
    pk = np.zeros((LANES, FOX_HK * HB), np.float32)
    ok = np.zeros((1, FOX_HK * HB), np.float32)
    for h in range(FOX_H):
        kh, g = divmod(h, FOX_G)
        for part in range(3):
            pq[16 * part + h, h * HB + 64 + part] = 1.0
            oq[0, h * HB + 67 + 3 * g + part] = 1.0
            pk[16 * part + h, kh * HB + 67 + 3 * g + part] = -1.0
    for kh in range(FOX_HK):
        ok[0, kh * HB + 64: kh * HB + 67] = 1.0
    return (jnp.asarray(pq, BF16), jnp.asarray(pk, BF16), jnp.asarray(oq), jnp.asarray(ok))


def _head_mm_kernel(a_ref, w_ref, o_ref):
    o_ref[...] = _dot(a_ref[...], w_ref[0]).astype(o_ref.dtype)


def head_mm(a, w, out_dtype):
    n = a.shape[0]
    nh, ka, kb = w.shape
    return pl.pallas_call(
        _head_mm_kernel, grid=(nh,),
        in_specs=[pl.BlockSpec((n, ka), lambda h: (0, h)), pl.BlockSpec((1, ka, kb), lambda h: (h, 0, 0))],
        out_specs=pl.BlockSpec((n, kb), lambda h: (0, h)),
        out_shape=jax.ShapeDtypeStruct((n, nh * kb), out_dtype),
        compiler_params=_cparams(("parallel",)), name="head_mm")(a, w)


def _page_group_copies(pt_ref, layer, b, group, slot, srcs, bufs, sems):
    cps = []
    for j in range(PG):
        phys = pt_ref[b, group * PG + j]
        for a, (src, buf) in enumerate(zip(srcs, bufs)):
            cps.append(pltpu.make_async_copy(src.at[layer, phys], buf.at[slot, j], sems.at[slot, a]))
    return cps


def _fetch_page_groups(pt_ref, layer, steps, group_of_step, srcs, bufs, sems):
    b = pl.program_id(0)
    p = pl.program_id(1)
    t = b * steps + p
    slot = t % 2
    total = pl.num_programs(0) * steps

    @pl.when(t == 0)
    def _():
        for cp in _page_group_copies(pt_ref, layer, b, group_of_step(p), slot, srcs, bufs, sems):
            cp.start()

    @pl.when(t + 1 < total)
    def _():
        wrap = p + 1 == steps
        nb = jnp.where(wrap, b + 1, b)
        np_ = jnp.where(wrap, 0, p + 1)
        for cp in _page_group_copies(pt_ref, layer, nb, group_of_step(np_), 1 - slot, srcs, bufs, sems):
            cp.start()

    for cp in _page_group_copies(pt_ref, layer, b, group_of_step(p), slot, srcs, bufs, sems):
        cp.wait()
    return slot


def _mla_dec_kernel(pt_ref, qa_ref, qp_ref, qn_ref, kn_ref, cn_ref, wukt_ref, gpe_ref, cos_ref, sin_ref,
                    ckv_hbm, kpe_hbm, o_ref, cbuf, pbuf, sems, m_scr, l_scr, acc_scr, *, layer, steps):
    p = pl.program_id(1)
    slot = _fetch_page_groups(pt_ref, layer, steps, lambda s: s, (ckv_hbm, kpe_hbm), (cbuf, pbuf), sems)
    ckv_refs = [cbuf.at[slot, j] for j in range(PG)]
    kpe_refs = [pbuf.at[slot, j] for j in range(PG)]

    @pl.when(p == 0)
    def _():
        s_new = jnp.sum(qn_ref[0].astype(F32) * kn_ref[0].astype(F32), axis=-1, keepdims=True)
        m_scr[...] = jnp.broadcast_to(s_new, m_scr.shape)
        l_scr[...] = jnp.ones_like(l_scr)
        acc_scr[...] = jnp.broadcast_to(cn_ref[0], acc_scr.shape)

    qa = qa_ref[0]
    qp = qp_ref[0]
    wukt = wukt_ref[...]
    half = MLA_ROPE // 2
    chunk = 2 * PAGE
    cb = jnp.concatenate([r[...] for r in ckv_refs], axis=0).astype(BF16)
    kt = jnp.concatenate([r[...] for r in kpe_refs], axis=1)
    nk = cb.shape[0]
    ss_parts = []
    for c in range(nk // chunk):
        knt = _dot_nt(wukt, cb[c * chunk:(c + 1) * chunk])
        ss_parts.append(jnp.sum((knt * knt).reshape(MLA_H, MLA_NOPE, chunk), axis=1))
    ss = jnp.concatenate(ss_parts, axis=1) + jnp.sum(kt * kt, axis=0, keepdims=True)
    r = lax.rsqrt(ss * (1.0 / MLA_QK) + EPS)
    kg = kt * gpe_ref[...]
    cos, sin = cos_ref[...], sin_ref[...]
    k1, k2 = kg[:half], kg[half:]
    rk = jnp.concatenate([k1 * cos - k2 * sin, k1 * sin + k2 * cos], axis=0).astype(BF16)
    s = (_dot_nt(qa, cb) + _dot(qp, rk)) * r
    m_old = m_scr[...]
    m_new = jnp.maximum(m_old, jnp.max(s, axis=-1, keepdims=True))
    alpha = jnp.exp(m_old - m_new)
    pr = jnp.exp(s - m_new[:, :1])
    l_scr[...] = l_scr[...] * alpha + jnp.sum(pr, axis=-1, keepdims=True)
    acc_scr[...] = acc_scr[...] * alpha[:, :1] + _dot(pr.astype(BF16), cb)
    m_scr[...] = m_new

    @pl.when(p == pl.num_programs(1) - 1)
    def _():
        o_ref[0] = (acc_scr[...] / l_scr[:, :1]).astype(o_ref.dtype)


def mla_decode(page_table, layer, q_abs, q_pe, q_new, k_new, ckv_new, wukt, gpe_col, cos_t, sin_t,
               cache_ckv, cache_kpe_t):
    db, n_pages = page_table.shape
    steps = n_pages // PG
    per_b = lambda n: pl.BlockSpec((1, MLA_H, n), lambda b, p, pt: (b, 0, 0))
    full = lambda a: pl.BlockSpec(a.shape, lambda b, p, pt: (0,) * a.ndim)
    tab = pl.BlockSpec((MLA_ROPE // 2, PG * PAGE), lambda b, p, pt: (0, p))
    hbm = pl.BlockSpec(memory_space=pl.ANY)
    grid_spec = pltpu.PrefetchScalarGridSpec(
        num_scalar_prefetch=1, grid=(db, steps),
        in_specs=[per_b(KV_LORA), per_b(MLA_ROPE), per_b(HB), per_b(HB),
                  pl.BlockSpec((1, 1, KV_LORA), lambda b, p, pt: (b, 0, 0)),
                  full(wukt), full(gpe_col), tab, tab, hbm, hbm],
        out_specs=per_b(KV_LORA),
        scratch_shapes=[pltpu.VMEM((2, PG, PAGE, KV_LORA), F32), pltpu.VMEM((2, PG, MLA_ROPE, PAGE), F32),
                        pltpu.SemaphoreType.DMA((2, 2)),
                        pltpu.VMEM((MLA_H, LANES), F32), pltpu.VMEM((MLA_H, LANES), F32),
                        pltpu.VMEM((MLA_H, KV_LORA), F32)])
    return pl.pallas_call(
        functools.partial(_mla_dec_kernel, layer=layer, steps=steps), grid_spec=grid_spec,
        out_shape=jax.ShapeDtypeStruct((db, MLA_H, KV_LORA), BF16),
        compiler_params=_cparams(("arbitrary", "arbitrary")), name="mla_decode")(
            page_table, q_abs, q_pe, q_new, k_new, ckv_new, wukt, gpe_col, cos_t, sin_t,
            cache_ckv, cache_kpe_t)


def _fox_dec_kernel(pt_ref, qbd_ref, q_ref, kn_ref, vn_ref, fq_ref, u_ref, k_hbm, v_hbm, lf_hbm, o_ref,
                    kbuf, vbuf, lfbuf, sems, m_scr, l_scr, acc_scr, carry, *, layer, steps):
    p = pl.program_id(1)
    slot = _fetch_page_groups(pt_ref, layer, steps, lambda s: steps - 1 - s,
                              (k_hbm, v_hbm, lf_hbm), (kbuf, vbuf, lfbuf), sems)
    k_refs = [kbuf.at[slot, j] for j in range(PG)]
    v_refs = [vbuf.at[slot, j] for j in range(PG)]
    lf_refs = [lfbuf.at[slot, j] for j in range(PG)]
    nkv = FOX_HK * FOX_HD

    @pl.when(p == 0)
    def _():
        s_new = jnp.sum(q_ref[0].astype(F32) * kn_ref[0].astype(F32), axis=-1, keepdims=True)
        m_scr[...] = jnp.broadcast_to(s_new, m_scr.shape)
        l_scr[...] = jnp.ones_like(l_scr)
        acc_scr[...] = jnp.broadcast_to(vn_ref[0], acc_scr.shape)
        carry[...] = jnp.zeros_like(carry)

    qbd = qbd_ref[0]
    fq = fq_ref[0]
    u = u_ref[...]
    lf_all = jnp.concatenate([r[...] for r in lf_refs], axis=0)
    hi = lf_all.astype(BF16)
    lo = (lf_all - hi.astype(F32)).astype(BF16)
    ext = _dot(hi, u) + _dot(lo, u)
    c = carry[...]
    sb = [None] * PG
    for j in range(PG - 1, -1, -1):
        e = ext[j * FOX_H:(j + 1) * FOX_H]
        kt = k_refs[j][...].reshape(nkv, PAGE).astype(BF16)
        sb[j] = _dot(qbd, kt) + e[:, :PAGE] + c + fq
        c = c + e[:, PAGE:]
    carry[...] = c
    mx = sb[0]
    for j in range(1, PG):
        mx = jnp.maximum(mx, sb[j])
    m_old = m_scr[...]
    m_new = jnp.maximum(m_old, jnp.max(mx, axis=-1, keepdims=True))
    alpha = jnp.exp(m_old - m_new)
    psum = None
    pv = None
    for j in range(PG):
        pr = jnp.exp(sb[j] - m_new)
        psum = pr if psum is None else psum + pr
        vt = v_refs[j][...].reshape(nkv, PAGE).astype(BF16)
        d = _dot_nt(pr.astype(BF16), vt)
        pv = d if pv is None else pv + d
    l_scr[...] = l_scr[...] * alpha + jnp.sum(psum, axis=-1, keepdims=True)
    acc_scr[...] = acc_scr[...] * alpha[:, :1] + pv
    m_scr[...] = m_new

    @pl.when(p == pl.num_programs(1) - 1)
    def _():
        o_ref[0] = (acc_scr[...] / l_scr[:, :1]).astype(o_ref.dtype)


def fox_decode(page_table, layer, q_bd, q16, kn16, vn_flat, fq, cache_kt, cache_vt, cache_lft):
    db, n_pages = page_table.shape
    steps = n_pages // PG
    nkv = FOX_HK * FOX_HD
    u = np.zeros((PAGE, PAGE + LANES), np.float32)
    u[:, :PAGE] = (np.arange(PAGE)[:, None] > np.arange(PAGE)[None, :])
    u[:, PAGE:] = 1.0
    u = jnp.asarray(u, BF16)
    per_b = lambda n: pl.BlockSpec((1, FOX_H, n), lambda b, p, pt: (b, 0, 0))
    full = lambda a: pl.BlockSpec(a.shape, lambda b, p, pt: (0,) * a.ndim)

    hbm = pl.BlockSpec(memory_space=pl.ANY)
    grid_spec = pltpu.PrefetchScalarGridSpec(
        num_scalar_prefetch=1, grid=(db, steps),
        in_specs=[per_b(nkv), per_b(FOX_HD), per_b(FOX_HD),
                  pl.BlockSpec((1, 1, nkv), lambda b, p, pt: (b, 0, 0)), per_b(LANES), full(u),
                  hbm, hbm, hbm],
        out_specs=per_b(nkv),
        scratch_shapes=[pltpu.VMEM((2, PG, FOX_HK, FOX_HD, PAGE), F32),
                        pltpu.VMEM((2, PG, FOX_HK, FOX_HD, PAGE), F32),
                        pltpu.VMEM((2, PG, FOX_H, PAGE), F32),
                        pltpu.SemaphoreType.DMA((2, 3)),
                        pltpu.VMEM((FOX_H, LANES), F32), pltpu.VMEM((FOX_H, LANES), F32),
                        pltpu.VMEM((FOX_H, nkv), F32), pltpu.VMEM((FOX_H, LANES), F32)])
    return pl.pallas_call(
        functools.partial(_fox_dec_kernel, layer=layer, steps=steps), grid_spec=grid_spec,
        out_shape=jax.ShapeDtypeStruct((db, FOX_H, nkv), F32),
        compiler_params=_cparams(("arbitrary", "arbitrary")), name="fox_decode")(
            page_table, q_bd, q16, kn16, vn_flat, fq, u, cache_kt, cache_vt, cache_lft)


def _mla_weights(w_in, w_qup, w_uk, w_uv, g_qn, g_kn, w_o):
    pe_lay = jnp.zeros((D_MODEL, HB), F32).at[:, MLA_ROPE_LANES].set(w_in[:, 640:672])
    w_ext = jnp.concatenate([w_in[:, :640], pe_lay, w_in[:, 640:672]], axis=1).astype(BF16)
    wq = _pad_heads(w_qup, MLA_H, MLA_QK, MLA_LANE).astype(BF16)
    nope_lane = MLA_LANE[:MLA_NOPE]
    wk = _pad_heads(w_uk, MLA_H, MLA_NOPE, nope_lane).astype(BF16)
    wv = _pad_heads(w_uv, MLA_H, MLA_V, np.arange(MLA_V)).astype(BF16)
    wo = _pad_heads(w_o.T, MLA_H, MLA_V, np.arange(MLA_V)).T.astype(BF16)
    gq_lay = _pad_vec(g_qn, MLA_LANE)
    gk_lay = _pad_vec(g_kn, MLA_LANE)
    wabs = w_uk.reshape(KV_LORA, MLA_H, MLA_NOPE).transpose(1, 2, 0) * g_kn[None, :MLA_NOPE, None]
    wabs = jnp.zeros((MLA_H, HB, KV_LORA), F32).at[:, nope_lane, :].set(wabs).astype(BF16)
    wuv_h = w_uv.reshape(KV_LORA, MLA_H, MLA_V).transpose(1, 0, 2)
    wuv_h = jnp.concatenate([wuv_h, jnp.zeros_like(wuv_h)], axis=-1).astype(BF16)
    wukt = w_uk.T.astype(BF16)
    gpe_col = g_kn[MLA_NOPE:].reshape(MLA_ROPE, 1)
    return dict(w_ext=w_ext, wq=wq, wk=wk, wv=wv, wo=wo, gq_lay=gq_lay, gk_lay=gk_lay, wabs=wabs,
                wuv_h=wuv_h, wukt=wukt, gpe_col=gpe_col)


def _fox_weights(w_in, b_f, g_qn, g_kn, w_o):
    nq, nk = FOX_H * FOX_HD, FOX_HK * FOX_HD
    ar = np.arange(FOX_HD)
    wq = (_pad_heads(w_in[:, :nq], FOX_H, FOX_HD, ar)).astype(BF16)
    wk = _pad_heads(w_in[:, nq:nq + nk], FOX_HK, FOX_HD, ar).astype(BF16)
    wv = _pad_heads(w_in[:, nq + nk:nq + 2 * nk], FOX_HK, FOX_HD, ar).astype(BF16)
    wg = _pad_heads(w_in[:, nq + 2 * nk:2 * nq + 2 * nk], FOX_H, FOX_HD, ar).astype(BF16)
    wf = w_in[:, 2 * nq + 2 * nk:]
    wf3 = jnp.concatenate([wf, wf, wf, jnp.zeros((D_MODEL, LANES - 3 * FOX_H), F32)], axis=1).astype(BF16)
    bf3 = jnp.concatenate([b_f, b_f, b_f, jnp.zeros((LANES - 3 * FOX_H,), F32)]).reshape(1, LANES)
    wo = _pad_heads(w_o.T, FOX_H, FOX_HD, ar).T.astype(BF16)
    gq_lay = _pad_vec(g_qn, ar) * FOX_SCALE
    gk_lay = _pad_vec(g_kn, ar)
    return dict(wq=wq, wk=wk, wv=wv, wg=wg, wf3=wf3, bf3=bf3, wo=wo, gq_lay=gq_lay, gk_lay=gk_lay)


def _unpad_heads(a, n_heads, hd):
    return a.reshape(a.shape[0], n_heads, HB)[:, :, :hd]


def kernel(x_prompt, x_sample, cache_mla_ckv, cache_mla_kpe, cache_fox_k, cache_fox_v, cache_fox_logf, page_table, norm_mix, norm_ffn, mla_w_in, mla_g_qlat, mla_g_kvlat, mla_w_qup, mla_w_uk, mla_w_uv, mla_g_qn, mla_g_kn, mla_w_o, fox_w_in, fox_b_f, fox_g_qn, fox_g_kn, fox_w_o, ffn_w_gu, ffn_w_down, moe_w_router, moe_w_gu, moe_w_down):
    bsz, seq, _ = x_prompt.shape
    db, tdec, _ = x_sample.shape
    assert tdec == 1
    n_pages = page_table.shape[1]
    past = n_pages * PAGE
    depth = norm_mix.shape[0]
    tp = 512 if seq % 512 == 0 else seq
    ts = db
    xp = x_prompt.reshape(bsz * seq, D_MODEL)
    xs = x_sample.reshape(db, D_MODEL)

    cp, sp = _rope_tables(jnp.arange(seq))
    cs, ss = _rope_tables(jnp.full((db,), past))
    half = MLA_ROPE // 2
    inv = 1.0 / (ROPE_THETA ** (jnp.arange(half, dtype=F32) / half))
    ang = inv[:, None] * jnp.arange(past, dtype=F32)[None, :]
    cos_t, sin_t = jnp.cos(ang), jnp.sin(ang)
    place_q, place_k, ones_q, ones_k = _fox_bias_tables()
    kv_sel = jnp.asarray((np.arange(FOX_H)[:, None] // FOX_G == np.arange(FOX_HK)[None, :]), F32)
    mla_kpe_t = jnp.transpose(cache_mla_kpe, (0, 1, 3, 2))
    fox_kt = jnp.transpose(cache_fox_k, (0, 1, 3, 4, 2))
    fox_vt = jnp.transpose(cache_fox_v, (0, 1, 3, 4, 2))
    fox_lft = jnp.transpose(cache_fox_logf, (0, 1, 3, 2))

    outs = {k: [] for k in ("p_ckv", "p_kpe", "s_ckv", "s_kpe", "p_fk", "p_fv", "p_flf", "s_fk", "s_fv", "s_flf")}
    for i in range(depth):
        j = i // 2
        g_mix = norm_mix[i].reshape(1, D_MODEL)
        g_ffn = norm_ffn[i].reshape(1, D_MODEL)
        if i % 2 == 0:
            w = _mla_weights(mla_w_in[j], mla_w_qup[j], mla_w_uk[j], mla_w_uv[j], mla_g_qn[j], mla_g_kn[j],
                             mla_w_o[j])
            gql, gkvl = mla_g_qlat[j].reshape(1, -1), mla_g_kvlat[j].reshape(1, -1)
            cq, ckv, kpl, kpe = mla_in(xp, g_mix, w["w_ext"], gql, gkvl, tp)
            q = mla_q(cq, w["wq"], w["gq_lay"], cp, sp, tp)
            k, v = mla_kv(ckv, kpl, w["wk"], w["wv"], w["gk_lay"], cp, sp, tp)
            o = flash_attention(q, k, v, bsz, seq, MLA_H, MLA_H, tp)
            xp = out_proj(xp, o, w["wo"], tp)
            outs["p_ckv"].append(ckv.reshape(bsz, seq, KV_LORA))
            outs["p_kpe"].append(kpe.reshape(bsz, seq, MLA_ROPE))
            cq, ckv, kpl, kpe = mla_in(xs, g_mix, w["w_ext"], gql, gkvl, ts)
            q = mla_q(cq, w["wq"], w["gq_lay"], cs, ss, ts)
            k, _ = mla_kv(ckv, kpl, w["wk"], w["wv"], w["gk_lay"], cs, ss, ts)
            q_abs = head_mm(q, w["wabs"], BF16).reshape(db, MLA_H, KV_LORA)
            q3 = q.reshape(db, MLA_H, HB)
            q_pe = q3[:, :, MLA_ROPE_LANES]
            o_lat = mla_decode(page_table, j, q_abs, q_pe, q3, k.reshape(db, MLA_H, HB),
                               ckv.reshape(db, 1, KV_LORA), w["wukt"], w["gpe_col"], cos_t, sin_t,
                               cache_mla_ckv, mla_kpe_t)
            o = head_mm(o_lat.reshape(db, MLA_H * KV_LORA), w["wuv_h"], BF16)
            xs = out_proj(xs, o, w["wo"], ts)
            outs["s_ckv"].append(ckv.reshape(db, 1, KV_LORA))
            outs["s_kpe"].append(kpe.reshape(db, 1, MLA_ROPE))
        else:
            w = _fox_weights(fox_w_in[j], fox_b_f[j], fox_g_qn[j], fox_g_kn[j], fox_w_o[j])
            lf, fparts = fox_lf(xp, g_mix, w["wf3"], w["bf3"], bsz, seq, tp, True)
            q, k, kf, v, vf, gate = fox_proj(xp, g_mix, w["wq"], w["wk"], w["wv"], w["wg"], w["gq_lay"],
                                             w["gk_lay"], fparts, place_q, place_k, ones_q, ones_k, tp)
            o = flash_attention(q, k, v, bsz, seq, FOX_H, FOX_HK, tp)
            xp = out_proj(xp, o, w["wo"], tp, gate=gate)
            outs["p_fk"].append(_unpad_heads(kf, FOX_HK, FOX_HD).reshape(bsz, seq, FOX_HK, FOX_HD))
            outs["p_fv"].append(_unpad_heads(vf, FOX_HK, FOX_HD).reshape(bsz, seq, FOX_HK, FOX_HD))
            outs["p_flf"].append(lf[:, :FOX_H].reshape(bsz, seq, FOX_H))
            lf, fparts = fox_lf(xs, g_mix, w["wf3"], w["bf3"], 1, db, ts, False)
            q, k, kf, v, vf, gate = fox_proj(xs, g_mix, w["wq"], w["wk"], w["wv"], w["wg"], w["gq_lay"],
                                             w["gk_lay"], fparts, place_q, place_k, ones_q, ones_k, ts)
            q16 = _unpad_heads(q, FOX_H, FOX_HD)
            kf4 = _unpad_heads(kf, FOX_HK, FOX_HD)
            vf4 = _unpad_heads(vf, FOX_HK, FOX_HD)
            kn16 = jnp.repeat(kf4, FOX_G, axis=1)
            fq = jnp.broadcast_to(lf[:, :FOX_H, None], (db, FOX_H, LANES))
            q_bd = (q16[:, :, None, :] * kv_sel[None, :, :, None].astype(BF16)).reshape(db, FOX_H, -1)
            o_all = fox_decode(page_table, j, q_bd, q16, kn16, vf4.reshape(db, 1, -1), fq,
                               fox_kt, fox_vt, fox_lft)
            o16 = jnp.sum(o_all.reshape(db, FOX_H, FOX_HK, FOX_HD) * kv_sel[None, :, :, None], axis=2)
            o = jnp.concatenate([o16, jnp.zeros_like(o16)], axis=-1).reshape(db, FOX_H * HB).astype(BF16)
            xs = out_proj(xs, o, w["wo"], ts, gate=gate)
            outs["s_fk"].append(kf4.reshape(db, 1, FOX_HK, FOX_HD))
            outs["s_fv"].append(vf4.reshape(db, 1, FOX_HK, FOX_HD))
            outs["s_flf"].append(lf[:, :FOX_H].reshape(db, 1, FOX_H))
        if i % 2 == 0:
            wgu, wdn = ffn_w_gu[j].astype(BF16), ffn_w_down[j].astype(BF16)
            fc = wdn.shape[0] // 2
            xp = ffn_dense(xp, g_ffn, wgu, wdn, tp, fc)
            xs = ffn_dense(xs, g_ffn, wgu, wdn, ts, fc)
        else:
            wgu, wdn = moe_w_gu[j].astype(BF16), moe_w_down[j].astype(BF16)
            wr = jnp.concatenate([moe_w_router[j], jnp.zeros((D_MODEL, LANES - N_EXP), F32)], axis=1)
            xp = moe_layer(xp, g_ffn, wr, wgu, wdn, tp, tp, 512)
            xs = moe_layer(xs, g_ffn, wr, wgu, wdn, ts, ts, 512)
    st = lambda name: jnp.stack(outs[name])
    return (xp.reshape(bsz, seq, D_MODEL), xs.reshape(db, 1, D_MODEL),
            st("p_ckv"), st("p_kpe"), st("p_fk"), st("p_fv"), st("p_flf"),
            st("s_ckv"), st("s_kpe"), st("s_fk"), st("s_fv"), st("s_flf"))
```
